```python
import jax, jax.numpy as jnp
from jax import lax
import numpy as np

D_MODEL = 4096
BATCH = 8
SEQ = 2048
DEPTH = 4

CTX_LEN = 256
GRID_W = 64
HEAD_DIM = 128
A_HEADS = 16
A_KV_HEADS = 4
B_HEADS = 16
NA_KH_MAX = 8
NA_KW = 16
Q_BLOCK = 128
ROPE_THETA = 10000.0
MOD_RANK = 512
N_MOD = 6
N_EXPERTS = 64
TOP_K = 8
N_GROUPS = 8
TOPK_GROUPS = 4
EXPERT_HIDDEN = 128
SHARED_HIDDEN = 256
ROUTED_SCALE = 2.5
EPS = 1e-6

A_Q = A_HEADS * HEAD_DIM
A_KV = A_KV_HEADS * HEAD_DIM
B_W = B_HEADS * HEAD_DIM
COL_QA = 0
COL_QB = COL_QA + A_Q
COL_KA = COL_QB + B_W
COL_VA = COL_KA + A_KV
COL_KB = COL_VA + A_KV
COL_VB = COL_KB + B_W
COL_GA = COL_VB + B_W
COL_GB = COL_GA + D_MODEL
IN_COLS = COL_GB + D_MODEL
RPB_H = 2 * NA_KH_MAX - 1
RPB_W = 2 * NA_KW - 1

kernel_name = 'hybrid_gqa_natten_moe_diffusion_trunk'


def rmsnorm(x, g):
    xf = x.astype(jnp.float32)
    y = xf * lax.rsqrt(jnp.mean(xf * xf, axis=-1, keepdims=True) + EPS)
    return (y * g.astype(jnp.float32)).astype(x.dtype)


def modulation(cvec, w_down, w_up, b_up):
    m = (jax.nn.silu(cvec) @ w_down) @ w_up + b_up
    return m.reshape(m.shape[:-1] + (N_MOD, D_MODEL))


def split_heads(a, n):
    return a.reshape(a.shape[:-1] + (n, HEAD_DIM))


def rope_axis(x, pos):
    half = x.shape[-1] // 2
    inv_freq = ROPE_THETA ** (-jnp.arange(half, dtype=jnp.float32) / half)
    ang = pos.astype(jnp.float32)[:, None] * inv_freq[None, :]
    cos = jnp.cos(ang)[:, None, :]
    sin = jnp.sin(ang)[:, None, :]
    xf = x.astype(jnp.float32)
    x1, x2 = xf[..., :half], xf[..., half:]
    return jnp.concatenate([x1 * cos - x2 * sin, x2 * cos + x1 * sin], axis=-1).astype(x.dtype)


def rope_2d(x, row, col):
    half = x.shape[-1] // 2
    return jnp.concatenate([rope_axis(x[..., :half], row), rope_axis(x[..., half:], col)], axis=-1)


def attend(q, k, v):
    b, tq, hq, d = q.shape
    hkv = k.shape[2]
    qg = q.reshape(b, tq, hkv, hq // hkv, d)
    s = jnp.einsum('bqhgd,bkhd->bhgqk', qg, k, preferred_element_type=jnp.float32) * (d ** -0.5)
    p = jax.nn.softmax(s, axis=-1).astype(v.dtype)
    o = jnp.einsum('bhgqk,bkhd->bqhgd', p, v)
    return o.reshape(b, tq, hq * d)


def blocked_attend(q, k, v):
    b, t, hq, d = q.shape
    n_blk = t // Q_BLOCK
    qb = q.reshape(b, n_blk, Q_BLOCK, hq, d).swapaxes(0, 1)
    o = lax.map(lambda qq: attend(qq, k, v), qb)
    return o.swapaxes(0, 1).reshape(b, t, hq * d)


def neighbourhood_attend(q, k, v, k_ctx, v_ctx, rpb, rows):
    b, t, h, d = q.shape
    kh = min(NA_KH_MAX, rows)
    kg = k.reshape(b, rows, GRID_W, h, d)
    vg = v.reshape(b, rows, GRID_W, h, d)
    qg = q.reshape(b, rows, GRID_W, h, d).swapaxes(0, 1)
    col = jnp.arange(GRID_W)
    col_start = jnp.clip(col - NA_KW // 2, 0, GRID_W - NA_KW)
    col_ok = (col[None, :] >= col_start[:, None]) & (col[None, :] < col_start[:, None] + NA_KW)
    mask = jnp.broadcast_to(col_ok[:, None, :], (GRID_W, kh, GRID_W)).reshape(GRID_W, kh * GRID_W)
    dc_idx = jnp.clip(col[None, :] - col[:, None] + NA_KW - 1, 0, RPB_W - 1)
    row_start = jnp.clip(jnp.arange(rows) - kh // 2, 0, rows - kh)
    scale = d ** -0.5
    rpb32 = rpb.astype(jnp.float32)

    def one_row(args):
        r, rs, qr = args
        kr = lax.dynamic_slice_in_dim(kg, rs, kh, axis=1).reshape(b, kh * GRID_W, h, d)
        vr = lax.dynamic_slice_in_dim(vg, rs, kh, axis=1).reshape(b, kh * GRID_W, h, d)
        dr_idx = rs + jnp.arange(kh) - r + NA_KH_MAX - 1
        bias = rpb32[:, dr_idx[:, None, None], dc_idx[None, :, :]]
        bias = bias.transpose(0, 2, 1, 3).reshape(h, GRID_W, kh * GRID_W)
        s_nb = jnp.einsum('bqhd,bkhd->bhqk', qr, kr, preferred_element_type=jnp.float32) * scale + bias
        s_nb = jnp.where(mask, s_nb, -jnp.inf)
        s_cx = jnp.einsum('bqhd,bkhd->bhqk', qr, k_ctx, preferred_element_type=jnp.float32) * scale
        p = jax.nn.softmax(jnp.concatenate([s_cx, s_nb], axis=-1), axis=-1).astype(v.dtype)
        o = jnp.einsum('bhqk,bkhd->bqhd', p, jnp.concatenate([v_ctx, vr], axis=1))
        return o.reshape(b, GRID_W, h * d)

    o = lax.map(one_row, (jnp.arange(rows), row_start, qg))
    return o.swapaxes(0, 1).reshape(b, t, h * d)


def merge_branches(oa, ob, ga, gb, w_branch_a, w_branch_b, w_out):
    y = jax.nn.sigmoid(ga) * (oa @ w_branch_a) + jax.nn.sigmoid(gb) * (ob @ w_branch_b)
    return y @ w_out


def moe_ffn(h, w_router, b_router, w_exp_gate, w_exp_up, w_exp_down, w_sh_gate, w_sh_up, w_sh_down):
    n = h.shape[0]
    per_group = N_EXPERTS // N_GROUPS
    scores = jax.nn.sigmoid((h @ w_router).astype(jnp.float32))
    choice = scores + b_router.astype(jnp.float32)
    grp_score = lax.top_k(choice.reshape(n, N_GROUPS, per_group), 2)[0].sum(-1)
    _, grp_idx = lax.top_k(grp_score, TOPK_GROUPS)
    grp_mask = jax.nn.one_hot(grp_idx, N_GROUPS, dtype=jnp.float32).sum(1)
    expert_ok = jnp.repeat(grp_mask, per_group, axis=-1) > 0
    _, top_idx = lax.top_k(jnp.where(expert_ok, choice, -jnp.inf), TOP_K)
    top_w = jnp.take_along_axis(scores, top_idx, axis=-1)
    top_w = top_w / jnp.sum(top_w, axis=-1, keepdims=True) * ROUTED_SCALE
    gates = jnp.sum(jax.nn.one_hot(top_idx, N_EXPERTS, dtype=jnp.float32) * top_w[..., None], axis=1)
    hid = jax.nn.silu(jnp.einsum('nd,edh->neh', h, w_exp_gate)) * jnp.einsum('nd,edh->neh', h, w_exp_up)
    routed = jnp.einsum('neh,ehd->nd', hid * gates[..., None].astype(hid.dtype), w_exp_down)
    shared = (jax.nn.silu(h @ w_sh_gate) * (h @ w_sh_up)) @ w_sh_down
    return routed + shared


def trunk_layer(x, ctx, c, c_ctx, row, col, rows, g_attn, g_ffn, w_mod_down, w_mod_up, b_mod_up,
                w_in, q_norm, k_norm, rpb, w_branch_a, w_branch_b, w_out, w_router, b_router,
                w_exp_gate, w_exp_up, w_exp_down, w_sh_gate, w_sh_up, w_sh_down, update_ctx):
    t = x.shape[1]
    tc = ctx.shape[1]
    mod_x = modulation(c, w_mod_down, w_mod_up, b_mod_up)
    mod_c = modulation(c_ctx, w_mod_down, w_mod_up, b_mod_up)
    sh1, sc1, gt1, sh2, sc2, gt2 = [mod_x[:, i][:, None, :] for i in range(N_MOD)]
    csh1, csc1, cgt1, csh2, csc2, cgt2 = [mod_c[i] for i in range(N_MOD)]

    hx = rmsnorm(x, g_attn) * (1 + sc1) + sh1
    hc = rmsnorm(ctx, g_attn) * (1 + csc1) + csh1
    px = hx @ w_in
    qa = rope_2d(rmsnorm(split_heads(px[..., COL_QA:COL_QB], A_HEADS), q_norm), row, col)
    ka = rope_2d(rmsnorm(split_heads(px[..., COL_KA:COL_VA], A_KV_HEADS), k_norm), row, col)
    va = split_heads(px[..., COL_VA:COL_KB], A_KV_HEADS)
    qb = split_heads(px[..., COL_QB:COL_KA], B_HEADS)
    kb = split_heads(px[..., COL_KB:COL_VB], B_HEADS)
    vb = split_heads(px[..., COL_VB:COL_GA], B_HEADS)
    ga = px[..., COL_GA:COL_GB]
    gb = px[..., COL_GB:IN_COLS]

    if update_ctx:
        pc = hc @ w_in
        pc_kv = pc[..., COL_KA:COL_GA]
    else:
        pc_kv = hc @ w_in[:, COL_KA:COL_GA]
    o_va, o_kb, o_vb = COL_VA - COL_KA, COL_KB - COL_KA, COL_VB - COL_KA
    ka_c = rmsnorm(split_heads(pc_kv[..., :o_va], A_KV_HEADS), k_norm)
    va_c = split_heads(pc_kv[..., o_va:o_kb], A_KV_HEADS)
    kb_c = split_heads(pc_kv[..., o_kb:o_vb], B_HEADS)
    vb_c = split_heads(pc_kv[..., o_vb:], B_HEADS)

    oa = blocked_attend(qa, jnp.concatenate([ka_c, ka], axis=1), jnp.concatenate([va_c, va], axis=1))
    ob = neighbourhood_attend(qb, kb, vb, kb_c, vb_c, rpb, rows)
    x = x + gt1 * merge_branches(oa, ob, ga, gb, w_branch_a, w_branch_b, w_out)

    if update_ctx:
        qa_c = rmsnorm(split_heads(pc[..., COL_QA:COL_QB], A_HEADS), q_norm)
        qb_c = split_heads(pc[..., COL_QB:COL_KA], B_HEADS)
        oa_c = attend(qa_c, ka_c, va_c)
        ob_c = attend(qb_c, kb_c, vb_c)
        ctx = ctx + cgt1 * merge_branches(oa_c, ob_c, pc[..., COL_GA:COL_GB], pc[..., COL_GB:IN_COLS],
                                          w_branch_a, w_branch_b, w_out)

    h2 = rmsnorm(x, g_ffn) * (1 + sc2) + sh2
    if update_ctx:
        h2c = rmsnorm(ctx, g_ffn) * (1 + csc2) + csh2
        h2 = jnp.concatenate([h2c, h2], axis=1)
    f = lax.map(lambda hb: moe_ffn(hb, w_router, b_router, w_exp_gate, w_exp_up, w_exp_down,
                                   w_sh_gate, w_sh_up, w_sh_down), h2)
    x = x + gt2 * f[:, f.shape[1] - t:]
    if update_ctx:
        ctx = ctx + cgt2 * f[:, :tc]
    return x, ctx


def setup_inputs(seed: int = 0) -> dict:
    key = jax.random.key(seed)
    ks = jax.random.split(key, 26)
    f32 = jnp.float32

    def nrm(k, shape, scale):
        return jax.random.normal(k, shape, f32) * scale

    return {
        'x': nrm(ks[0], (BATCH, SEQ, D_MODEL), 1.0),
        'c': nrm(ks[1], (BATCH, D_MODEL), 1.0),
        'ctx': nrm(ks[2], (BATCH, CTX_LEN, D_MODEL), 1.0),
        'c_ctx': nrm(ks[3], (D_MODEL,), 1.0),
        'g_attn': 1.0 + nrm(ks[4], (DEPTH, D_MODEL), 0.05),
        'g_ffn': 1.0 + nrm(ks[5], (DEPTH, D_MODEL), 0.05),
        'w_mod_down': nrm(ks[6], (DEPTH, D_MODEL, MOD_RANK), D_MODEL ** -0.5),
        'w_mod_up': nrm(ks[7], (DEPTH, MOD_RANK, N_MOD * D_MODEL), 0.3 * MOD_RANK ** -0.5),
        'b_mod_up': nrm(ks[8], (DEPTH, N_MOD * D_MODEL), 0.02),
        'w_in': nrm(ks[9], (DEPTH, D_MODEL, IN_COLS), D_MODEL ** -0.5),
        'q_norm': 1.0 + nrm(ks[10], (DEPTH, HEAD_DIM), 0.05),
        'k_norm': 1.0 + nrm(ks[11], (DEPTH, HEAD_DIM), 0.05),
        'rpb': nrm(ks[12], (DEPTH, B_HEADS, RPB_H, RPB_W), 0.1),
        'w_branch_a': nrm(ks[13], (DEPTH, A_Q, D_MODEL), A_Q ** -0.5),
        'w_branch_b': nrm(ks[14], (DEPTH, B_W, D_MODEL), B_W ** -0.5),
        'w_out': nrm(ks[15], (DEPTH, D_MODEL, D_MODEL), D_MODEL ** -0.5),
        'w_router': nrm(ks[16], (DEPTH, D_MODEL, N_EXPERTS), D_MODEL ** -0.5),
        'b_router': nrm(ks[17], (DEPTH, N_EXPERTS), 0.01),
        'w_exp_gate': nrm(ks[18], (DEPTH, N_EXPERTS, D_MODEL, EXPERT_HIDDEN), D_MODEL ** -0.5),
        'w_exp_up': nrm(ks[19], (DEPTH, N_EXPERTS, D_MODEL, EXPERT_HIDDEN), D_MODEL ** -0.5),
        'w_exp_down': nrm(ks[20], (DEPTH, N_EXPERTS, EXPERT_HIDDEN, D_MODEL), EXPERT_HIDDEN ** -0.5),
        'w_sh_gate': nrm(ks[21], (DEPTH, D_MODEL, SHARED_HIDDEN), D_MODEL ** -0.5),
        'w_sh_up': nrm(ks[22], (DEPTH, D_MODEL, SHARED_HIDDEN), D_MODEL ** -0.5),
        'w_sh_down': nrm(ks[23], (DEPTH, SHARED_HIDDEN, D_MODEL), SHARED_HIDDEN ** -0.5),
        'g_final': 1.0 + nrm(ks[24], (D_MODEL,), 0.05),
    }


def reference(x, c, ctx, c_ctx, g_attn, g_ffn, w_mod_down, w_mod_up, b_mod_up, w_in, q_norm, k_norm,
              rpb, w_branch_a, w_branch_b, w_out, w_router, b_router, w_exp_gate, w_exp_up, w_exp_down,
              w_sh_gate, w_sh_up, w_sh_down, g_final):
    t = x.shape[1]
    rows = t // GRID_W
    pos = jnp.arange(t)
    row = pos // GRID_W
    col = pos % GRID_W
    for l in range(DEPTH):
        x, ctx = trunk_layer(x, ctx, c, c_ctx, row, col, rows, g_attn[l], g_ffn[l], w_mod_down[l],
                             w_mod_up[l], b_mod_up[l], w_in[l], q_norm[l], k_norm[l], rpb[l],
                             w_branch_a[l], w_branch_b[l], w_out[l], w_router[l], b_router[l],
                             w_exp_gate[l], w_exp_up[l], w_exp_down[l], w_sh_gate[l], w_sh_up[l],
                             w_sh_down[l], l < DEPTH - 1)
    return rmsnorm(x, g_final)
```

```python
import functools

import jax
import jax.numpy as jnp
from jax import lax
from jax.experimental import pallas as pl
from jax.experimental.pallas import tpu as pltpu

GRID_W = 64
HEAD_DIM = 128
A_HEADS = 16
A_KV_HEADS = 4
B_HEADS = 16
NA_KH_MAX = 8
NA_KW = 16
ROPE_THETA = 10000.0
N_MOD = 6
TOP_K = 8
N_GROUPS = 8
TOPK_GROUPS = 4
ROUTED_SCALE = 2.5
EPS = 1e-6

F32 = jnp.float32
BF16 = jnp.bfloat16
HIGHEST = lax.Precision.HIGHEST
NEG_INF = float("-inf")

VMEM_LIMIT_BYTES = 56 * 2**20
MOD_ROWS = 16
GATE_LANES = 128


def _params(*sem):
    return pltpu.CompilerParams(dimension_semantics=sem, vmem_limit_bytes=VMEM_LIMIT_BYTES)


def _dot(a, b):
    return jnp.dot(a, b, preferred_element_type=F32)


def _dot_nt(a, b):
    return lax.dot_general(a, b, (((1,), (1,)), ((), ())), preferred_element_type=F32)


def _sigmoid(x):
    return 1.0 / (1.0 + jnp.exp(-x))


def _mod_down_kernel(c_ref, w_ref, o_ref):
    c = c_ref[...]
    o_ref[...] = jnp.dot(c * _sigmoid(c), w_ref[...], preferred_element_type=F32, precision=HIGHEST)


def _mod_up_kernel(t_ref, w_ref, b_ref, o_ref):
    o_ref[...] = jnp.dot(t_ref[...], w_ref[...], preferred_element_type=F32, precision=HIGHEST) + b_ref[...]


def modulation_all(cvec, w_down, w_up, b_up, tn=2048):
    depth, d, r = w_down.shape
    n = w_up.shape[2]
    t = pl.pallas_call(
        _mod_down_kernel,
        grid=(depth,),
        in_specs=[pl.BlockSpec((MOD_ROWS, d), lambda l: (0, 0)),
                  pl.BlockSpec((None, d, r), lambda l: (l, 0, 0))],
        out_specs=pl.BlockSpec((None, MOD_ROWS, r), lambda l: (l, 0, 0)),
        out_shape=jax.ShapeDtypeStruct((depth, MOD_ROWS, r), F32),
        compiler_params=_params("arbitrary"),
        name="mod_down",
    )(cvec, w_down)
    return pl.pallas_call(
        _mod_up_kernel,
        grid=(depth, n // tn),
        in_specs=[pl.BlockSpec((None, MOD_ROWS, r), lambda l, j: (l, 0, 0)),
                  pl.BlockSpec((None, r, tn), lambda l, j: (l, 0, j)),
                  pl.BlockSpec((None, 1, tn), lambda l, j: (l, 0, j))],
        out_specs=pl.BlockSpec((None, MOD_ROWS, tn), lambda l, j: (l, 0, j)),
        out_shape=jax.ShapeDtypeStruct((depth, MOD_ROWS, n), F32),
        compiler_params=_params("arbitrary", "arbitrary"),
        name="mod_up",
    )(t, w_up, b_up.reshape(depth, 1, n))


class Rows:
    def __init__(self, batch, ctx_len, seq):
        self.batch, self.ctx_len, self.seq = batch, ctx_len, seq
        self.nc = batch * ctx_len
        self.nx = batch * seq
        self.n = self.nc + self.nx

    def mod_index(self, row):
        return jnp.where(row < self.nc, self.batch, (row - self.nc) // self.seq)


def _rmsnorm_rows(x, g):
    ms = jnp.mean(x * x, axis=-1, keepdims=True)
    return x * lax.rsqrt(ms + EPS) * g


def _norm_mod_kernel(x_ref, g_ref, sc_ref, sh_ref, o_ref):
    y = _rmsnorm_rows(x_ref[...], g_ref[...])
    o_ref[...] = (y * (1.0 + sc_ref[...]) + sh_ref[...]).astype(o_ref.dtype)


def norm_mod(rows, x, g, sc, sh, row0, tm=256):
    n, d = x.shape
    t0 = row0 // tm
    mod_map = lambda i: (rows.mod_index((i + t0) * tm), 0, 0)
    return pl.pallas_call(
        _norm_mod_kernel,
        grid=((n - row0) // tm,),
        in_specs=[pl.BlockSpec((tm, d), lambda i: (i + t0, 0)),
                  pl.BlockSpec((1, d), lambda i: (0, 0)),
                  pl.BlockSpec((None, 1, d), mod_map),
                  pl.BlockSpec((None, 1, d), mod_map)],
        out_specs=pl.BlockSpec((tm, d), lambda i: (i + t0, 0)),
        out_shape=jax.ShapeDtypeStruct((n, d), BF16),
        compiler_params=_params("arbitrary"),
        name="norm_mod",
    )(x, g.reshape(1, d), sc, sh)


def _final_norm_kernel(x_ref, g_ref, o_ref):
    o_ref[...] = _rmsnorm_rows(x_ref[...], g_ref[...])


def final_norm(x, g, row0, tm=256):
    n, d = x.shape
    t0 = row0 // tm
    return pl.pallas_call(
        _final_norm_kernel,
        grid=((n - row0) // tm,),
        in_specs=[pl.BlockSpec((tm, d), lambda i: (i + t0, 0)),
                  pl.BlockSpec((1, d), lambda i: (0, 0))],
        out_specs=pl.BlockSpec((tm, d), lambda i: (i, 0)),
        out_shape=jax.ShapeDtypeStruct((n - row0, d), F32),
        compiler_params=_params("arbitrary"),
        name="final_norm",
    )(x, g.reshape(1, d))


def _mm_kernel(x_ref, w_ref, o_ref):
    o_ref[...] = _dot(x_ref[...], w_ref[...]).astype(o_ref.dtype)


def matmul(x, w, tm=1024, tn=1024):
    m, k = x.shape
    n = w.shape[1]
    return pl.pallas_call(
        _mm_kernel,
        grid=(m // tm, n // tn),
        in_specs=[pl.BlockSpec((tm, k), lambda i, j: (i, 0)),
                  pl.BlockSpec((k, tn), lambda i, j: (0, j))],
        out_specs=pl.BlockSpec((tm, tn), lambda i, j: (i, j)),
        out_shape=jax.ShapeDtypeStruct((m, n), BF16),
        compiler_params=_params("arbitrary", "arbitrary"),
        name="matmul",
    )(x, w)


def rope_tables(rows, tm):
    half = HEAD_DIM // 2
    quarter = half // 2
    pos = jnp.arange(rows.seq)
    inv_freq = ROPE_THETA ** (-jnp.arange(quarter, dtype=F32) / quarter)
    ang_r = (pos // GRID_W).astype(F32)[:, None] * inv_freq[None, :]
    ang_c = (pos % GRID_W).astype(F32)[:, None] * inv_freq[None, :]
    ang = jnp.concatenate([ang_r, ang_r, ang_c, ang_c], axis=-1)
    sign = jnp.concatenate([-jnp.ones(quarter), jnp.ones(quarter)] * 2).astype(F32)
    cos = jnp.concatenate([jnp.ones((tm, HEAD_DIM), F32), jnp.cos(ang)], axis=0)
    sin = jnp.concatenate([jnp.zeros((tm, HEAD_DIM), F32), jnp.sin(ang) * sign[None, :]], axis=0)
    return cos, sin


def _qk_prep_kernel(q_ref, k_ref, cos_ref, sin_ref, qg_ref, kg_ref, qo_ref, ko_ref):
    cos = cos_ref[...]
    sin = sin_ref[...]
    quarter = HEAD_DIM // 4
    lane = lax.broadcasted_iota(jnp.int32, cos.shape, 1)
    first = (lane & quarter) == 0

    def prep(x_ref, g, o_ref):
        for h in range(x_ref.shape[1] // HEAD_DIM):
            sl = slice(h * HEAD_DIM, (h + 1) * HEAD_DIM)
            y = _rmsnorm_rows(x_ref[:, sl].astype(F32), g)
            partner = jnp.where(first, pltpu.roll(y, HEAD_DIM - quarter, 1), pltpu.roll(y, quarter, 1))
            o_ref[:, sl] = (y * cos + partner * sin).astype(o_ref.dtype)

    prep(q_ref, qg_ref[...], qo_ref)
    prep(k_ref, kg_ref[...], ko_ref)


def qk_prep(rows, px, q_cols, k_cols, q_norm, k_norm, cos, sin, tm=256):
    n = px.shape[0]
    (q0, qw), (k0, kw) = q_cols, k_cols
    per_seq = rows.seq // tm

    def tab_map(i):
        r = i * tm
        return (jnp.where(r < rows.nc, 0, 1 + ((r - rows.nc) // tm) % per_seq), 0)

    return pl.pallas_call(
        _qk_prep_kernel,
        grid=(n // tm,),
        in_specs=[pl.BlockSpec((tm, qw), lambda i: (i, q0 // qw)),
                  pl.BlockSpec((tm, kw), lambda i: (i, k0 // kw)),
                  pl.BlockSpec((tm, HEAD_DIM), tab_map),
                  pl.BlockSpec((tm, HEAD_DIM), tab_map),
                  pl.BlockSpec((1, HEAD_DIM), lambda i: (0, 0)),
                  pl.BlockSpec((1, HEAD_DIM), lambda i: (0, 0))],
        out_specs=[pl.BlockSpec((tm, qw), lambda i: (i, 0)),
                   pl.BlockSpec((tm, kw), lambda i: (i, 0))],
        out_shape=[jax.ShapeDtypeStruct((n, qw), BF16), jax.ShapeDtypeStruct((n, kw), BF16)],
        compiler_params=_params("arbitrary"),
        name="qk_prep",
    )(px, px, cos, sin, q_norm.reshape(1, HEAD_DIM), k_norm.reshape(1, HEAD_DIM))


def _softmax_pv(scores, values, scale):
    m = functools.reduce(jnp.maximum, [jnp.max(s, axis=-1, keepdims=True) for s in scores])
    ps = [jnp.exp((s - m) * scale) for s in scores]
    l = functools.reduce(jnp.add, [jnp.sum(p, axis=-1, keepdims=True) for p in ps])
    o = functools.reduce(jnp.add, [_dot(p.astype(v.dtype), v) for p, v in zip(ps, values)])
    return o / l


def _attn_ctx_kernel(q_ref, k_ref, v_ref, prev_ref, o_ref, *, nq, nk, scale):
    del prev_ref
    for h in range(nq):
        kh = h * nk // nq
        ks = slice(kh * HEAD_DIM, (kh + 1) * HEAD_DIM)
        qs = slice(h * HEAD_DIM, (h + 1) * HEAD_DIM)
        s = _dot_nt(q_ref[:, qs], k_ref[:, ks])
        o_ref[:, qs] = _softmax_pv([s], [v_ref[:, ks]], scale).astype(o_ref.dtype)


def attn_ctx(rows, q_arr, q0, k_arr, k0, v_arr, v0, out_prev, n_heads, n_kv_heads, nq=4):
    nk = nq * n_kv_heads // n_heads
    qw, kw = nq * HEAD_DIM, nk * HEAD_DIM
    cl = rows.ctx_len
    return pl.pallas_call(
        functools.partial(_attn_ctx_kernel, nq=nq, nk=nk, scale=HEAD_DIM ** -0.5),
        grid=(rows.batch, n_heads // nq),
        in_specs=[pl.BlockSpec((cl, qw), lambda b, g: (b, q0 // qw + g)),
                  pl.BlockSpec((cl, kw), lambda b, g: (b, k0 // kw + g)),
                  pl.BlockSpec((cl, kw), lambda b, g: (b, v0 // kw + g)),
                  pl.BlockSpec(memory_space=pl.ANY)],
        out_specs=pl.BlockSpec((cl, qw), lambda b, g: (b, g)),
        out_shape=jax.ShapeDtypeStruct(out_prev.shape, out_prev.dtype),
        input_output_aliases={3: 0},
        compiler_params=_params("arbitrary", "arbitrary"),
        name="attn_ctx",
    )(q_arr, k_arr, v_arr, out_prev)


def _attn_a_kernel(q_ref, kc_ref, kl_ref, vc_ref, vl_ref, o_ref, *, nq, scale):
    kc, kl, vc, vl = kc_ref[...], kl_ref[...], vc_ref[...], vl_ref[...]
    for h in range(nq):
        qs = slice(h * HEAD_DIM, (h + 1) * HEAD_DIM)
        q = q_ref[:, qs]
        o = _softmax_pv([_dot_nt(q, kc), _dot_nt(q, kl)], [vc, vl], scale)
        o_ref[:, qs] = o.astype(o_ref.dtype)


def attn_a(rows, qa, ka, px, v0, tq=512):
    nq = A_HEADS // A_KV_HEADS
    qw = nq * HEAD_DIM
    n_q_blocks = rows.seq // tq
    lat0 = rows.nc // rows.seq
    return pl.pallas_call(
        functools.partial(_attn_a_kernel, nq=nq, scale=HEAD_DIM ** -0.5),
        grid=(rows.batch, A_KV_HEADS, n_q_blocks),
        in_specs=[pl.BlockSpec((tq, qw), lambda b, g, i: (rows.nc // tq + b * n_q_blocks + i, g)),
                  pl.BlockSpec((rows.ctx_len, HEAD_DIM), lambda b, g, i: (b, g)),
                  pl.BlockSpec((rows.seq, HEAD_DIM), lambda b, g, i: (lat0 + b, g)),
                  pl.BlockSpec((rows.ctx_len, HEAD_DIM), lambda b, g, i: (b, v0 // HEAD_DIM + g)),
                  pl.BlockSpec((rows.seq, HEAD_DIM), lambda b, g, i: (lat0 + b, v0 // HEAD_DIM + g))],
        out_specs=pl.BlockSpec((tq, qw), lambda b, g, i: (rows.nc // tq + b * n_q_blocks + i, g)),
        out_shape=jax.ShapeDtypeStruct((rows.n, A_HEADS * HEAD_DIM), BF16),
        compiler_params=_params("arbitrary", "arbitrary", "arbitrary"),
        name="attn_a",
    )(qa, ka, ka, px, px)


NB_QROWS = 8
NB_KROWS = 16


def _nb_slab_start(i, grid_rows):
    lo = i * NB_QROWS - NA_KH_MAX // 2
    hi = grid_rows - NB_KROWS
    if isinstance(i, int):
        return min(max(lo, 0), hi)
    return jnp.clip(lo, 0, hi)


def _nb_bias_kernel(rpb_ref, o_ref, *, grid_rows):
    h = pl.program_id(0)
    w = GRID_W
    rpb_h, rpb_w = 2 * NA_KH_MAX - 1, 2 * NA_KW - 1
    qc = lax.broadcasted_iota(jnp.int32, (w, w), 0)
    kc = lax.broadcasted_iota(jnp.int32, (w, w), 1)
    cs = jnp.clip(qc - NA_KW // 2, 0, w - NA_KW)
    col_ok = (kc >= cs) & (kc < cs + NA_KW)
    dc = kc - qc + NA_KW - 1
    dc_masks = [dc == j for j in range(rpb_w)]
    tiles = []
    for dr in range(rpb_h):
        t = jnp.zeros((w, w), F32)
        for j in range(rpb_w):
            t = jnp.where(dc_masks[j], rpb_ref[(h * rpb_h + dr) * rpb_w + j], t)
        tiles.append(jnp.where(col_ok, t, NEG_INF))
    blank = jnp.full((w, w), NEG_INF, F32)
    for i in range(grid_rows // NB_QROWS):
        k0 = _nb_slab_start(i, grid_rows)
        for qr in range(NB_QROWS):
            r = i * NB_QROWS + qr
            rs = min(max(r - NA_KH_MAX // 2, 0), grid_rows - NA_KH_MAX)
            row = []
            for kr in range(NB_KROWS):
                key_row = k0 + kr
                ok = rs <= key_row < rs + NA_KH_MAX
                row.append(tiles[key_row - r + NA_KH_MAX - 1] if ok else blank)
            o_ref[i, qr * w:(qr + 1) * w, :] = jnp.concatenate(row, axis=1)


def nb_bias(rpb, grid_rows):
    n_heads = rpb.shape[0]
    nqb = grid_rows // NB_QROWS
    tq, tk = NB_QROWS * GRID_W, NB_KROWS * GRID_W
    return pl.pallas_call(
        functools.partial(_nb_bias_kernel, grid_rows=grid_rows),
        grid=(n_heads,),
        in_specs=[pl.BlockSpec(memory_space=pltpu.SMEM)],
        out_specs=pl.BlockSpec((None, nqb, tq, tk), lambda h: (h, 0, 0, 0)),
        out_shape=jax.ShapeDtypeStruct((n_heads, nqb, tq, tk), F32),
        compiler_params=_params("arbitrary"),
        name="nb_bias",
    )(rpb.reshape(-1))


def _attn_b_kernel(q_ref, kc_ref, kl_ref, vc_ref, vl_ref, bias_ref, o_ref, *, nh, scale, grid_rows):
    i = pl.program_id(1)
    tk = NB_KROWS * GRID_W
    start = pl.multiple_of(_nb_slab_start(i, grid_rows) * GRID_W, GRID_W)
    for h in range(nh):
        hs = slice(h * HEAD_DIM, (h + 1) * HEAD_DIM)
        q = q_ref[:, hs]
        s_c = _dot_nt(q, kc_ref[:, hs])
        s_w = _dot_nt(q, kl_ref[pl.ds(start, tk), hs]) + bias_ref[h] * (1.0 / scale)
        o = _softmax_pv([s_c, s_w], [vc_ref[:, hs], vl_ref[pl.ds(start, tk), hs]], scale)
        o_ref[:, hs] = o.astype(o_ref.dtype)


def attn_b(rows, px, q0, k0, v0, bias, nh=2):
    grid_rows = rows.seq // GRID_W
    tq = NB_QROWS * GRID_W
    nqb = rows.seq // tq
    hw = nh * HEAD_DIM
    lat0 = rows.nc // rows.seq
    q_map = lambda g, i, b: (rows.nc // tq + b * nqb + i, q0 // hw + g)
    return pl.pallas_call(
        functools.partial(_attn_b_kernel, nh=nh, scale=HEAD_DIM ** -0.5, grid_rows=grid_rows),
        grid=(B_HEADS // nh, nqb, rows.batch),
        in_specs=[pl.BlockSpec((tq, hw), q_map),
                  pl.BlockSpec((rows.ctx_len, hw), lambda g, i, b: (b, k0 // hw + g)),
                  pl.BlockSpec((rows.seq, hw), lambda g, i, b: (lat0 + b, k0 // hw + g)),
                  pl.BlockSpec((rows.ctx_len, hw), lambda g, i, b: (b, v0 // hw + g)),
                  pl.BlockSpec((rows.seq, hw), lambda g, i, b: (lat0 + b, v0 // hw + g)),
                  pl.BlockSpec((nh, None, tq, NB_KROWS * GRID_W), lambda g, i, b: (g, i, 0, 0))],
        out_specs=pl.BlockSpec((tq, hw), lambda g, i, b: (rows.nc // tq + b * nqb + i, g)),
        out_shape=jax.ShapeDtypeStruct((rows.n, B_HEADS * HEAD_DIM), BF16),
        compiler_params=_params("arbitrary", "arbitrary", "arbitrary"),
        name="attn_b",
    )(px, px, px, px, px, bias)


def _merge_kernel(oa_ref, ob_ref, wa_ref, wb_ref, ga_ref, gb_ref, o_ref):
    a = _dot(oa_ref[...], wa_ref[...])
    b = _dot(ob_ref[...], wb_ref[...])
    y = _sigmoid(ga_ref[...].astype(F32)) * a + _sigmoid(gb_ref[...].astype(F32)) * b
    o_ref[...] = y.astype(o_ref.dtype)


def merge(oa, ob, wa, wb, px, ga0, gb0, row0, tm=1024, tn=512):
    n, k = oa.shape
    d = wa.shape[1]
    t0 = row0 // tm
    return pl.pallas_call(
        _merge_kernel,
        grid=((n - row0) // tm, d // tn),
        in_specs=[pl.BlockSpec((tm, k), lambda i, j: (i + t0, 0)),
                  pl.BlockSpec((tm, k), lambda i, j: (i + t0, 0)),
                  pl.BlockSpec((k, tn), lambda i, j: (0, j)),
                  pl.BlockSpec((k, tn), lambda i, j: (0, j)),
                  pl.BlockSpec((tm, tn), lambda i, j: (i + t0, ga0 // tn + j)),
                  pl.BlockSpec((tm, tn), lambda i, j: (i + t0, gb0 // tn + j))],
        out_specs=pl.BlockSpec((tm, tn), lambda i, j: (i + t0, j)),
        out_shape=jax.ShapeDtypeStruct((n, d), BF16),
        compiler_params=_params("arbitrary", "arbitrary"),
        name="merge",
    )(oa, ob, wa, wb, px, px)


def _resid_mm_kernel(y_ref, w_ref, x_ref, g_ref, o_ref):
    o_ref[...] = x_ref[...] + g_ref[...] * _dot(y_ref[...], w_ref[...])


def resid_matmul(rows, y, w, x, gate, row0, tm, tn=512):
    n, k = y.shape
    d = w.shape[1]
    t0 = row0 // tm
    return pl.pallas_call(
        _resid_mm_kernel,
        grid=((n - row0) // tm, d // tn),
        in_specs=[pl.BlockSpec((tm, k), lambda i, j: (i + t0, 0)),
                  pl.BlockSpec((k, tn), lambda i, j: (0, j)),
                  pl.BlockSpec((tm, tn), lambda i, j: (i + t0, j)),
                  pl.BlockSpec((None, 1, tn), lambda i, j: (rows.mod_index((i + t0) * tm), 0, j))],
        out_specs=pl.BlockSpec((tm, tn), lambda i, j: (i + t0, j)),
        out_shape=jax.ShapeDtypeStruct(x.shape, x.dtype),
        input_output_aliases={2: 0},
        compiler_params=_params("arbitrary", "arbitrary"),
        name="resid_matmul",
    )(y, w, x, gate)


def _route(logits, bias, n_experts):
    lane = lax.broadcasted_iota(jnp.int32, logits.shape, 1)
    lane_f = lane.astype(F32)
    per_group = n_experts // N_GROUPS
    valid = lane < n_experts
    scores = _sigmoid(logits)
    choice = jnp.where(valid, scores + bias, NEG_INF)
    big = float(GATE_LANES)

    def first_argmax(x):
        m = jnp.max(x, axis=-1, keepdims=True)
        idx = jnp.min(jnp.where(x == m, lane_f, big), axis=-1, keepdims=True)
        return m, idx

    in_group = [(lane >= g * per_group) & (lane < (g + 1) * per_group) for g in range(N_GROUPS)]
    group_score = []
    for g in range(N_GROUPS):
        xg = jnp.where(in_group[g], choice, NEG_INF)
        m1, i1 = first_argmax(xg)
        m2 = jnp.max(jnp.where(lane_f == i1, NEG_INF, xg), axis=-1, keepdims=True)
        group_score.append(m1 + m2)
    ok = jnp.zeros(logits.shape, F32)
    for g in range(N_GROUPS):
        ahead = jnp.zeros(group_score[g].shape, F32)
        for o in range(N_GROUPS):
            if o != g:
                wins = group_score[o] >= group_score[g] if o < g else group_score[o] > group_score[g]
                ahead = ahead + jnp.where(wins, 1.0, 0.0)
        ok = jnp.where(in_group[g], jnp.where(ahead < TOPK_GROUPS, 1.0, 0.0), ok)
    x = jnp.where(ok > 0.5, choice, NEG_INF)
    chosen = jnp.zeros(logits.shape, F32)
    for _ in range(TOP_K):
        _, idx = first_argmax(x)
        hit = lane_f == idx
        chosen = jnp.where(hit, 1.0, chosen)
        x = jnp.where(hit, NEG_INF, x)
    top_w = jnp.where(chosen > 0.5, scores, 0.0)
    gates = top_w / jnp.sum(top_w, axis=-1, keepdims=True) * ROUTED_SCALE
    return jnp.where(lane == n_experts, 1.0, gates)


def _norm_router_kernel(x_ref, g_ref, sc_ref, sh_ref, wr_ref, br_ref, h_ref, gate_ref, *, n_experts):
    y = _rmsnorm_rows(x_ref[...], g_ref[...])
    h = y * (1.0 + sc_ref[...]) + sh_ref[...]
    h_ref[...] = h.astype(h_ref.dtype)
    logits = jnp.dot(h, wr_ref[...], preferred_element_type=F32, precision=HIGHEST)
    gate_ref[...] = _route(logits, br_ref[...], n_experts)


def norm_router(rows, x, g, sc, sh, w_router, b_router, row0, tm=256):
    n, d = x.shape
    e = w_router.shape[1]
    t0 = row0 // tm
    wr = jnp.zeros((d, GATE_LANES), F32).at[:, :e].set(w_router)
    br = jnp.zeros((1, GATE_LANES), F32).at[0, :e].set(b_router)
    mod_map = lambda i: (rows.mod_index((i + t0) * tm), 0, 0)
    return pl.pallas_call(
        functools.partial(_norm_router_kernel, n_experts=e),
        grid=((n - row0) // tm,),
        in_specs=[pl.BlockSpec((tm, d), lambda i: (i + t0, 0)),
                  pl.BlockSpec((1, d), lambda i: (0, 0)),
                  pl.BlockSpec((None, 1, d), mod_map),
                  pl.BlockSpec((None, 1, d), mod_map),
                  pl.BlockSpec((d, GATE_LANES), lambda i: (0, 0)),
                  pl.BlockSpec((1, GATE_LANES), lambda i: (0, 0))],
        out_specs=[pl.BlockSpec((tm, d), lambda i: (i + t0, 0)),
                   pl.BlockSpec((tm, GATE_LANES), lambda i: (i + t0, 0))],
        out_shape=[jax.ShapeDtypeStruct((n, d), BF16), jax.ShapeDtypeStruct((n, GATE_LANES), F32)],
        compiler_params=_params("arbitrary"),
        name="norm_router",
    )(x, g.reshape(1, d), sc, sh, wr, br)


def _moe_up_kernel(h_ref, wg_ref, wu_ref, gate_ref, o_ref, *, routed_cols, hidden_shift, n_experts):
    ck = o_ref.shape[1]
    h = h_ref[...]
    g = _dot(h, wg_ref[...])
    u = _dot(h, wu_ref[...])
    col = pl.program_id(1) * ck + lax.broadcasted_iota(jnp.int32, (GATE_LANES, ck), 1)
    row = lax.broadcasted_iota(jnp.int32, (GATE_LANES, ck), 0)
    owner = jnp.where(col < routed_cols, col >> hidden_shift, n_experts)
    expand = (row == owner).astype(BF16)
    gate = gate_ref[...]
    hi = gate.astype(BF16)
    lo = (gate - hi.astype(F32)).astype(BF16)
    gx = _dot(hi, expand) + _dot(lo, expand)
    o_ref[...] = ((g * _sigmoid(g)) * u * gx).astype(o_ref.dtype)


def moe_up(h, wg, wu, gates, routed_cols, expert_hidden, n_experts, row0, tm=512, ck=768):
    n, d = h.shape
    hc = wg.shape[1]
    t0 = row0 // tm
    shift = expert_hidden.bit_length() - 1
    assert 1 << shift == expert_hidden
    return pl.pallas_call(
        functools.partial(_moe_up_kernel, routed_cols=routed_cols, hidden_shift=shift, n_experts=n_experts),
        grid=((n - row0) // tm, hc // ck),
        in_specs=[pl.BlockSpec((tm, d), lambda i, j: (i + t0, 0)),
                  pl.BlockSpec((d, ck), lambda i, j: (0, j)),
                  pl.BlockSpec((d, ck), lambda i, j: (0, j)),
                  pl.BlockSpec((tm, GATE_LANES), lambda i, j: (i + t0, 0))],
        out_specs=pl.BlockSpec((tm, ck), lambda i, j: (i + t0, j)),
        out_shape=jax.ShapeDtypeStruct((n, hc), BF16),
        compiler_params=_params("arbitrary", "arbitrary"),
        name="moe_up",
    )(h, wg, wu, gates)


def kernel(x, c, ctx, c_ctx, g_attn, g_ffn, w_mod_down, w_mod_up, b_mod_up, w_in, q_norm, k_norm, rpb, w_branch_a, w_branch_b, w_out, w_router, b_router, w_exp_gate, w_exp_up, w_exp_down, w_sh_gate, w_sh_up, w_sh_down, g_final):
    batch, seq, d = x.shape
    ctx_len = ctx.shape[1]
    depth = w_in.shape[0]
    n_experts, _, expert_hidden = w_exp_gate.shape[1:]
    rows = Rows(batch, ctx_len, seq)
    assert rows.nc % rows.seq == 0 and batch < MOD_ROWS

    a_q, a_kv, b_w = A_HEADS * HEAD_DIM, A_KV_HEADS * HEAD_DIM, B_HEADS * HEAD_DIM
    col_qa = 0
    col_qb = col_qa + a_q
    col_ka = col_qb + b_w
    col_va = col_ka + a_kv
    col_kb = col_va + a_kv
    col_vb = col_kb + b_w
    col_ga = col_vb + b_w
    col_gb = col_ga + d

    cvec = jnp.zeros((MOD_ROWS, d), F32).at[:batch].set(c).at[batch].set(c_ctx)
    mod = modulation_all(cvec, w_mod_down, w_mod_up, b_mod_up)

    prep_tm = 256
    cos, sin = rope_tables(rows, prep_tm)
    xs = jnp.concatenate([ctx.reshape(rows.nc, d), x.reshape(rows.nx, d)], axis=0)

    for l in range(depth):
        update_ctx = l < depth - 1
        row0 = 0 if update_ctx else rows.nc
        m = [mod[l, :batch + 1, k * d:(k + 1) * d][:, None, :] for k in range(N_MOD)]
        sh1, sc1, gt1, sh2, sc2, gt2 = m

        h1 = norm_mod(rows, xs, g_attn[l], sc1, sh1, 0)
        px = matmul(h1, w_in[l].astype(BF16))
        qa, ka = qk_prep(rows, px, (col_qa, a_q), (col_ka, a_kv), q_norm[l], k_norm[l], cos, sin, prep_tm)
        oa = attn_a(rows, qa, ka, px, col_va)
        ob = attn_b(rows, px, col_qb, col_kb, col_vb, nb_bias(rpb[l], seq // GRID_W))
        if update_ctx:
            oa = attn_ctx(rows, qa, 0, ka, 0, px, col_va, oa, A_HEADS, A_KV_HEADS)
            ob = attn_ctx(rows, px, col_qb, px, col_kb, px, col_vb, ob, B_HEADS, B_HEADS)
        y = merge(oa, ob, w_branch_a[l].astype(BF16), w_branch_b[l].astype(BF16), px, col_ga, col_gb, row0)
        xs = resid_matmul(rows, y, w_out[l].astype(BF16), xs, gt1, row0, tm=1024)

        h2, gates = norm_router(rows, xs, g_ffn[l], sc2, sh2, w_router[l], b_router[l], row0)
        routed_cols = n_experts * expert_hidden
        wg = jnp.concatenate([w_exp_gate[l].transpose(1, 0, 2).reshape(d, routed_cols), w_sh_gate[l]], axis=1)
        wu = jnp.concatenate([w_exp_up[l].transpose(1, 0, 2).reshape(d, routed_cols), w_sh_up[l]], axis=1)
        wd = jnp.concatenate([w_exp_down[l].reshape(routed_cols, d), w_sh_down[l]], axis=0)
        hid = moe_up(h2, wg.astype(BF16), wu.astype(BF16), gates, routed_cols, expert_hidden, n_experts, row0)
        xs = resid_matmul(rows, hid, wd.astype(BF16), xs, gt2, row0, tm=512)

    return final_norm(xs, g_final, rows.nc).reshape(batch, seq, d)
```

```python
import functools
import math

import jax
import jax.numpy as jnp
from jax import lax
from jax.experimental import pallas as pl
from jax.experimental.pallas import tpu as pltpu

GRID_W = 64
HEAD_DIM = 128
A_HEADS = 16
A_KV_HEADS = 4
B_HEADS = 16
NA_KH_MAX = 8
NA_KW = 16
ROPE_THETA = 10000.0
N_MOD = 6
TOP_K = 8
N_GROUPS = 8
TOPK_GROUPS = 4
ROUTED_SCALE = 2.5
EPS = 1e-6

F32 = jnp.float32
BF16 = jnp.bfloat16
U32 = jnp.uint32
HIGHEST = lax.Precision.HIGHEST
NEG_INF = float("-inf")
LANES = 128

VMEM_LIMIT_BYTES = 56 * 2**20
MOD_ROWS = 16
MOE_TM = 256


def _params(*sem):
    return pltpu.CompilerParams(dimension_semantics=sem, vmem_limit_bytes=VMEM_LIMIT_BYTES)


def _dot(a, b):
    return jnp.dot(a, b, preferred_element_type=F32)


def _dot_nt(a, b):
    return lax.dot_general(a, b, (((1,), (1,)), ((), ())), preferred_element_type=F32)


def _sigmoid(x):
    return 1.0 / (1.0 + jnp.exp(-x))


def _mod_down_kernel(c_ref, w_ref, o_ref):
    c = c_ref[...]
    o_ref[...] = jnp.dot(c * _sigmoid(c), w_ref[...], preferred_element_type=F32, precision=HIGHEST)


def _mod_up_kernel(t_ref, w_ref, b_ref, o_ref):
    o_ref[...] = jnp.dot(t_ref[...], w_ref[...], preferred_element_type=F32, precision=HIGHEST) + b_ref[...]


def modulation_all(cvec, w_down, w_up, b_up, tn=2048):
    depth, d, r = w_down.shape
    n = w_up.shape[2]
    t = pl.pallas_call(
        _mod_down_kernel,
        grid=(depth,),
        in_specs=[pl.BlockSpec((MOD_ROWS, d), lambda l: (0, 0)),
                  pl.BlockSpec((None, d, r), lambda l: (l, 0, 0))],
        out_specs=pl.BlockSpec((None, MOD_ROWS, r), lambda l: (l, 0, 0)),
        out_shape=jax.ShapeDtypeStruct((depth, MOD_ROWS, r), F32),
        compiler_params=_params("arbitrary"),
        name="mod_down",
    )(cvec, w_down)
    return pl.pallas_call(
        _mod_up_kernel,
        grid=(depth, n // tn),
        in_specs=[pl.BlockSpec((None, MOD_ROWS, r), lambda l, j: (l, 0, 0)),
                  pl.BlockSpec((None, r, tn), lambda l, j: (l, 0, j)),
                  pl.BlockSpec((None, 1, tn), lambda l, j: (l, 0, j))],
        out_specs=pl.BlockSpec((None, MOD_ROWS, tn), lambda l, j: (l, 0, j)),
        out_shape=jax.ShapeDtypeStruct((depth, MOD_ROWS, n), F32),
        compiler_params=_params("arbitrary", "arbitrary"),
        name="mod_up",
    )(t, w_up, b_up.reshape(depth, 1, n))


class Rows:
    def __init__(self, batch, ctx_len, seq):
        self.batch, self.ctx_len, self.seq = batch, ctx_len, seq
        self.nc = batch * ctx_len
        self.nx = batch * seq
        self.n = self.nc + self.nx

    def mod_index(self, row):
        return jnp.where(row < self.nc, self.batch, (row - self.nc) // self.seq)


def _rmsnorm_rows(x, g):
    ms = jnp.mean(x * x, axis=-1, keepdims=True)
    return x * lax.rsqrt(ms + EPS) * g


def _norm_mod_kernel(x_ref, g_ref, sc_ref, sh_ref, o_ref):
    y = _rmsnorm_rows(x_ref[...], g_ref[...])
    o_ref[...] = (y * (1.0 + sc_ref[...]) + sh_ref[...]).astype(o_ref.dtype)


def norm_mod(rows, x, g, sc, sh, row0, tm=256):
    n, d = x.shape
    t0 = row0 // tm
    mod_map = lambda i: (rows.mod_index((i + t0) * tm), 0, 0)
    return pl.pallas_call(
        _norm_mod_kernel,
        grid=((n - row0) // tm,),
        in_specs=[pl.BlockSpec((tm, d), lambda i: (i + t0, 0)),
                  pl.BlockSpec((1, d), lambda i: (0, 0)),
                  pl.BlockSpec((None, 1, d), mod_map),
                  pl.BlockSpec((None, 1, d), mod_map)],
        out_specs=pl.BlockSpec((tm, d), lambda i: (i + t0, 0)),
        out_shape=jax.ShapeDtypeStruct((n, d), BF16),
        compiler_params=_params("arbitrary"),
        name="norm_mod",
    )(x, g.reshape(1, d), sc, sh)


def _final_norm_kernel(x_ref, g_ref, o_ref):
    o_ref[...] = _rmsnorm_rows(x_ref[...], g_ref[...])


def final_norm(x, g, row0, tm=256):
    n, d = x.shape
    t0 = row0 // tm
    return pl.pallas_call(
        _final_norm_kernel,
        grid=((n - row0) // tm,),
        in_specs=[pl.BlockSpec((tm, d), lambda i: (i + t0, 0)),
                  pl.BlockSpec((1, d), lambda i: (0, 0))],
        out_specs=pl.BlockSpec((tm, d), lambda i: (i, 0)),
        out_shape=jax.ShapeDtypeStruct((n - row0, d), F32),
        compiler_params=_params("arbitrary"),
        name="final_norm",
    )(x, g.reshape(1, d))


def _mm_kernel(x_ref, w_ref, o_ref):
    o_ref[...] = _dot(x_ref[...], w_ref[...]).astype(o_ref.dtype)


def matmul(x, w, l, tm=1024, tn=1024):
    m, k = x.shape
    n = w.shape[2]
    return pl.pallas_call(
        _mm_kernel,
        grid=(m // tm, n // tn),
        in_specs=[pl.BlockSpec((tm, k), lambda i, j: (i, 0)),
                  pl.BlockSpec((None, k, tn), lambda i, j: (l, 0, j))],
        out_specs=pl.BlockSpec((tm, tn), lambda i, j: (i, j)),
        out_shape=jax.ShapeDtypeStruct((m, n), BF16),
        compiler_params=_params("arbitrary", "arbitrary"),
        name="matmul",
    )(x, w)


def rope_tables(rows, tm):
    half = HEAD_DIM // 2
    quarter = half // 2
    pos = jnp.arange(rows.seq)
    inv_freq = ROPE_THETA ** (-jnp.arange(quarter, dtype=F32) / quarter)
    ang_r = (pos // GRID_W).astype(F32)[:, None] * inv_freq[None, :]
    ang_c = (pos % GRID_W).astype(F32)[:, None] * inv_freq[None, :]
    ang = jnp.concatenate([ang_r, ang_r, ang_c, ang_c], axis=-1)
    sign = jnp.concatenate([-jnp.ones(quarter), jnp.ones(quarter)] * 2).astype(F32)
    cos = jnp.concatenate([jnp.ones((tm, HEAD_DIM), F32), jnp.cos(ang)], axis=0)
    sin = jnp.concatenate([jnp.zeros((tm, HEAD_DIM), F32), jnp.sin(ang) * sign[None, :]], axis=0)
    return cos, sin


def _qk_prep_kernel(q_ref, k_ref, cos_ref, sin_ref, qg_ref, kg_ref, qo_ref, ko_ref):
    cos = cos_ref[...]
    sin = sin_ref[...]
    quarter = HEAD_DIM // 4
    lane = lax.broadcasted_iota(jnp.int32, cos.shape, 1)
    first = (lane & quarter) == 0

    def prep(x_ref, g, o_ref):
        for h in range(x_ref.shape[1] // HEAD_DIM):
            sl = slice(h * HEAD_DIM, (h + 1) * HEAD_DIM)
            y = _rmsnorm_rows(x_ref[:, sl].astype(F32), g)
            partner = jnp.where(first, pltpu.roll(y, HEAD_DIM - quarter, 1), pltpu.roll(y, quarter, 1))
            o_ref[:, sl] = (y * cos + partner * sin).astype(o_ref.dtype)

    prep(q_ref, qg_ref[...], qo_ref)
    prep(k_ref, kg_ref[...], ko_ref)


def qk_prep(rows, px, q_cols, k_cols, q_norm, k_norm, cos, sin, tm=256):
    n = px.shape[0]
    (q0, qw), (k0, kw) = q_cols, k_cols
    per_seq = rows.seq // tm

    def tab_map(i):
        r = i * tm
        return (jnp.where(r < rows.nc, 0, 1 + ((r - rows.nc) // tm) % per_seq), 0)

    return pl.pallas_call(
        _qk_prep_kernel,
        grid=(n // tm,),
        in_specs=[pl.BlockSpec((tm, qw), lambda i: (i, q0 // qw)),
                  pl.BlockSpec((tm, kw), lambda i: (i, k0 // kw)),
                  pl.BlockSpec((tm, HEAD_DIM), tab_map),
                  pl.BlockSpec((tm, HEAD_DIM), tab_map),
                  pl.BlockSpec((1, HEAD_DIM), lambda i: (0, 0)),
                  pl.BlockSpec((1, HEAD_DIM), lambda i: (0, 0))],
        out_specs=[pl.BlockSpec((tm, qw), lambda i: (i, 0)),
                   pl.BlockSpec((tm, kw), lambda i: (i, 0))],
        out_shape=[jax.ShapeDtypeStruct((n, qw), BF16), jax.ShapeDtypeStruct((n, kw), BF16)],
        compiler_params=_params("arbitrary"),
        name="qk_prep",
    )(px, px, cos, sin, q_norm.reshape(1, HEAD_DIM), k_norm.reshape(1, HEAD_DIM))


def _softmax_pv(scores, values, scale):
    c = scale * math.log2(math.e)
    m = functools.reduce(jnp.maximum, [jnp.max(s, axis=-1, keepdims=True) for s in scores])
    ps = [jnp.exp2((s - m) * c) for s in scores]
    l = functools.reduce(jnp.add, [jnp.sum(p, axis=-1, keepdims=True) for p in ps])
    o = functools.reduce(jnp.add, [_dot(p.astype(v.dtype), v) for p, v in zip(ps, values)])
    return o / l


def _attn_ctx_kernel(q_ref, k_ref, v_ref, prev_ref, o_ref, *, nq, nk, scale):
    del prev_ref
    for h in range(nq):
        kh = h * nk // nq
        ks = slice(kh * HEAD_DIM, (kh + 1) * HEAD_DIM)
        qs = slice(h * HEAD_DIM, (h + 1) * HEAD_DIM)
        s = _dot_nt(q_ref[:, qs], k_ref[:, ks])
        o_ref[:, qs] = _softmax_pv([s], [v_ref[:, ks]], scale).astype(o_ref.dtype)


def attn_ctx(rows, q_arr, q0, k_arr, k0, v_arr, v0, out_prev, n_heads, n_kv_heads, nq=4):
    nk = nq * n_kv_heads // n_heads
    qw, kw = nq * HEAD_DIM, nk * HEAD_DIM
    cl = rows.ctx_len
    return pl.pallas_call(
        functools.partial(_attn_ctx_kernel, nq=nq, nk=nk, scale=HEAD_DIM ** -0.5),
        grid=(rows.batch, n_heads // nq),
        in_specs=[pl.BlockSpec((cl, qw), lambda b, g: (b, q0 // qw + g)),
                  pl.BlockSpec((cl, kw), lambda b, g: (b, k0 // kw + g)),
                  pl.BlockSpec((cl, kw), lambda b, g: (b, v0 // kw + g)),
                  pl.BlockSpec(memory_space=pl.ANY)],
        out_specs=pl.BlockSpec((cl, qw), lambda b, g: (b, g)),
        out_shape=jax.ShapeDtypeStruct(out_prev.shape, out_prev.dtype),
        input_output_aliases={3: 0},
        compiler_params=_params("arbitrary", "arbitrary"),
        name="attn_ctx",
    )(q_arr, k_arr, v_arr, out_prev)


def _attn_a_kernel(q_ref, kc_ref, kl_ref, vc_ref, vl_ref, o_ref, *, nq, scale):
    kc, kl, vc, vl = kc_ref[...], kl_ref[...], vc_ref[...], vl_ref[...]
    for h in range(nq):
        qs = slice(h * HEAD_DIM, (h + 1) * HEAD_DIM)
        q = q_ref[:, qs]
        o = _softmax_pv([_dot_nt(q, kc), _dot_nt(q, kl)], [vc, vl], scale)
        o_ref[:, qs] = o.astype(o_ref.dtype)


def attn_a(rows, qa, ka, px, v0, tq=512):
    nq = A_HEADS // A_KV_HEADS
    qw = nq * HEAD_DIM
    n_q_blocks = rows.seq // tq
    lat0 = rows.nc // rows.seq
    return pl.pallas_call(
        functools.partial(_attn_a_kernel, nq=nq, scale=HEAD_DIM ** -0.5),
        grid=(rows.batch, A_KV_HEADS, n_q_blocks),
        in_specs=[pl.BlockSpec((tq, qw), lambda b, g, i: (rows.nc // tq + b * n_q_blocks + i, g)),
                  pl.BlockSpec((rows.ctx_len, HEAD_DIM), lambda b, g, i: (b, g)),
                  pl.BlockSpec((rows.seq, HEAD_DIM), lambda b, g, i: (lat0 + b, g)),
                  pl.BlockSpec((rows.ctx_len, HEAD_DIM), lambda b, g, i: (b, v0 // HEAD_DIM + g)),
                  pl.BlockSpec((rows.seq, HEAD_DIM), lambda b, g, i: (lat0 + b, v0 // HEAD_DIM + g))],
        out_specs=pl.BlockSpec((tq, qw), lambda b, g, i: (rows.nc // tq + b * n_q_blocks + i, g)),
        out_shape=jax.ShapeDtypeStruct((rows.n, A_HEADS * HEAD_DIM), BF16),
        compiler_params=_params("arbitrary", "arbitrary", "arbitrary"),
        name="attn_a",
    )(qa, ka, ka, px, px)


NB_QROWS = 8
NB_KROWS = 16


def _nb_slab_start(i, grid_rows):
    lo = i * NB_QROWS - NA_KH_MAX // 2
    hi = grid_rows - NB_KROWS
    if isinstance(i, int):
        return min(max(lo, 0), hi)
    return jnp.clip(lo, 0, hi)


def _nb_bias_kernel(rpb_ref, o_ref, *, grid_rows, inv_scale):
    h = pl.program_id(0)
    w = GRID_W
    rpb_h, rpb_w = 2 * NA_KH_MAX - 1, 2 * NA_KW - 1
    qc = lax.broadcasted_iota(jnp.int32, (w, w), 0)
    kc = lax.broadcasted_iota(jnp.int32, (w, w), 1)
    cs = jnp.clip(qc - NA_KW // 2, 0, w - NA_KW)
    col_ok = (kc >= cs) & (kc < cs + NA_KW)
    dc = kc - qc + NA_KW - 1
    dc_masks = [dc == j for j in range(rpb_w)]
    tiles = []
    for dr in range(rpb_h):
        t = jnp.zeros((w, w), F32)
        for j in range(rpb_w):
            t = jnp.where(dc_masks[j], rpb_ref[(h * rpb_h + dr) * rpb_w + j], t)
        tiles.append(jnp.where(col_ok, t * inv_scale, NEG_INF))
    blank = jnp.full((w, w), NEG_INF, F32)
    for i in range(grid_rows // NB_QROWS):
        k0 = _nb_slab_start(i, grid_rows)
        for qr in range(NB_QROWS):
            r = i * NB_QROWS + qr
            rs = min(max(r - NA_KH_MAX // 2, 0), grid_rows - NA_KH_MAX)
            row = []
            for kr in range(NB_KROWS):
                key_row = k0 + kr
                ok = rs <= key_row < rs + NA_KH_MAX
                row.append(tiles[key_row - r + NA_KH_MAX - 1] if ok else blank)
            o_ref[i, qr * w:(qr + 1) * w, :] = jnp.concatenate(row, axis=1)


def nb_bias(rpb, grid_rows):
    n_heads = rpb.shape[0]
    nqb = grid_rows // NB_QROWS
    tq, tk = NB_QROWS * GRID_W, NB_KROWS * GRID_W
    return pl.pallas_call(
        functools.partial(_nb_bias_kernel, grid_rows=grid_rows, inv_scale=HEAD_DIM ** 0.5),
        grid=(n_heads,),
        in_specs=[pl.BlockSpec(memory_space=pltpu.SMEM)],
        out_specs=pl.BlockSpec((None, nqb, tq, tk), lambda h: (h, 0, 0, 0)),
        out_shape=jax.ShapeDtypeStruct((n_heads, nqb, tq, tk), F32),
        compiler_params=_params("arbitrary"),
        name="nb_bias",
    )(rpb.reshape(-1))


def _attn_b_kernel(q_ref, kc_ref, kl_ref, vc_ref, vl_ref, bias_ref, o_ref, *, nh, scale, grid_rows):
    i = pl.program_id(1)
    tk = NB_KROWS * GRID_W
    start = pl.multiple_of(_nb_slab_start(i, grid_rows) * GRID_W, GRID_W)
    for h in range(nh):
        hs = slice(h * HEAD_DIM, (h + 1) * HEAD_DIM)
        q = q_ref[:, hs]
        s_c = _dot_nt(q, kc_ref[:, hs])
        s_w = _dot_nt(q, kl_ref[pl.ds(start, tk), hs]) + bias_ref[h]
        o = _softmax_pv([s_c, s_w], [vc_ref[:, hs], vl_ref[pl.ds(start, tk), hs]], scale)
        o_ref[:, hs] = o.astype(o_ref.dtype)


def attn_b(rows, px, q0, k0, v0, bias, nh=2):
    grid_rows = rows.seq // GRID_W
    tq = NB_QROWS * GRID_W
    nqb = rows.seq // tq
    hw = nh * HEAD_DIM
    lat0 = rows.nc // rows.seq
    q_map = lambda g, i, b: (rows.nc // tq + b * nqb + i, q0 // hw + g)
    return pl.pallas_call(
        functools.partial(_attn_b_kernel, nh=nh, scale=HEAD_DIM ** -0.5, grid_rows=grid_rows),
        grid=(B_HEADS // nh, nqb, rows.batch),
        in_specs=[pl.BlockSpec((tq, hw), q_map),
                  pl.BlockSpec((rows.ctx_len, hw), lambda g, i, b: (b, k0 // hw + g)),
                  pl.BlockSpec((rows.seq, hw), lambda g, i, b: (lat0 + b, k0 // hw + g)),
                  pl.BlockSpec((rows.ctx_len, hw), lambda g, i, b: (b, v0 // hw + g)),
                  pl.BlockSpec((rows.seq, hw), lambda g, i, b: (lat0 + b, v0 // hw + g)),
                  pl.BlockSpec((nh, None, tq, NB_KROWS * GRID_W), lambda g, i, b: (g, i, 0, 0))],
        out_specs=pl.BlockSpec((tq, hw), lambda g, i, b: (rows.nc // tq + b * nqb + i, g)),
        out_shape=jax.ShapeDtypeStruct((rows.n, B_HEADS * HEAD_DIM), BF16),
        compiler_params=_params("arbitrary", "arbitrary", "arbitrary"),
        name="attn_b",
    )(px, px, px, px, px, bias)


def _merge_kernel(oa_ref, ob_ref, wa_ref, wb_ref, ga_ref, gb_ref, o_ref):
    a = _dot(oa_ref[...], wa_ref[...])
    b = _dot(ob_ref[...], wb_ref[...])
    y = _sigmoid(ga_ref[...].astype(F32)) * a + _sigmoid(gb_ref[...].astype(F32)) * b
    o_ref[...] = y.astype(o_ref.dtype)


def merge(oa, ob, wa, wb, l, px, ga0, gb0, row0, tm=1024, tn=512):
    n, k = oa.shape
    d = wa.shape[2]
    t0 = row0 // tm
    return pl.pallas_call(
        _merge_kernel,
        grid=((n - row0) // tm, d // tn),
        in_specs=[pl.BlockSpec((tm, k), lambda i, j: (i + t0, 0)),
                  pl.BlockSpec((tm, k), lambda i, j: (i + t0, 0)),
                  pl.BlockSpec((None, k, tn), lambda i, j: (l, 0, j)),
                  pl.BlockSpec((None, k, tn), lambda i, j: (l, 0, j)),
                  pl.BlockSpec((tm, tn), lambda i, j: (i + t0, ga0 // tn + j)),
                  pl.BlockSpec((tm, tn), lambda i, j: (i + t0, gb0 // tn + j))],
        out_specs=pl.BlockSpec((tm, tn), lambda i, j: (i + t0, j)),
        out_shape=jax.ShapeDtypeStruct((n, d), BF16),
        compiler_params=_params("arbitrary", "arbitrary"),
        name="merge",
    )(oa, ob, wa, wb, px, px)


def _resid_mm_kernel(y_ref, w_ref, x_ref, g_ref, o_ref):
    o_ref[...] = x_ref[...] + g_ref[...] * _dot(y_ref[...], w_ref[...])


def resid_matmul(rows, y, w, l, x, gate, row0, tm, tn=512):
    n, k = y.shape
    d = w.shape[2]
    t0 = row0 // tm
    return pl.pallas_call(
        _resid_mm_kernel,
        grid=((n - row0) // tm, d // tn),
        in_specs=[pl.BlockSpec((tm, k), lambda i, j: (i + t0, 0)),
                  pl.BlockSpec((None, k, tn), lambda i, j: (l, 0, j)),
                  pl.BlockSpec((tm, tn), lambda i, j: (i + t0, j)),
                  pl.BlockSpec((None, 1, tn), lambda i, j: (rows.mod_index((i + t0) * tm), 0, j))],
        out_specs=pl.BlockSpec((tm, tn), lambda i, j: (i + t0, j)),
        out_shape=jax.ShapeDtypeStruct(x.shape, x.dtype),
        input_output_aliases={2: 0},
        compiler_params=_params("arbitrary", "arbitrary"),
        name="resid_matmul",
    )(y, w, x, gate)


def _pack_words(v):
    half = v.shape[1] // 2
    bits = lax.bitcast_convert_type(v.astype(BF16).astype(F32), U32)
    return (bits[:, :half] >> 16) | bits[:, half:]


def _unpack_words(w):
    lo = lax.bitcast_convert_type(w << 16, F32)
    hi = lax.bitcast_convert_type(w & U32(0xFFFF0000), F32)
    return lo, hi


def _store_token_major(ref, words):
    rows, nch = words.shape[0], words.shape[1] // LANES
    for c in range(nch):
        ref[pl.ds(c, rows, stride=nch), :] = words[:, c * LANES:(c + 1) * LANES]


def _load_token_major(ref, rows, nch):
    return [ref[pl.ds(c, rows, stride=nch), :] for c in range(nch)]


def _route(logits, bias, n_experts):
    lane = lax.broadcasted_iota(jnp.int32, logits.shape, 1)
    lane_f = lane.astype(F32)
    per_group = n_experts // N_GROUPS
    valid = lane < n_experts
    scores = _sigmoid(logits)
    choice = jnp.where(valid, scores + bias, NEG_INF)
    big = float(LANES)

    def first_argmax(x):
        m = jnp.max(x, axis=-1, keepdims=True)
        idx = jnp.min(jnp.where(x == m, lane_f, big), axis=-1, keepdims=True)
        return m, idx

    in_group = [(lane >= g * per_group) & (lane < (g + 1) * per_group) for g in range(N_GROUPS)]
    group_score = []
    for g in range(N_GROUPS):
        xg = jnp.where(in_group[g], choice, NEG_INF)
        m1, i1 = first_argmax(xg)
        m2 = jnp.max(jnp.where(lane_f == i1, NEG_INF, xg), axis=-1, keepdims=True)
        group_score.append(m1 + m2)
    ok = jnp.zeros(logits.shape, F32)
    for g in range(N_GROUPS):
        ahead = jnp.zeros(group_score[g].shape, F32)
        for o in range(N_GROUPS):
            if o != g:
                wins = group_score[o] >= group_score[g] if o < g else group_score[o] > group_score[g]
                ahead = ahead + jnp.where(wins, 1.0, 0.0)
        ok = jnp.where(in_group[g], jnp.where(ahead < TOPK_GROUPS, 1.0, 0.0), ok)
    x = jnp.where(ok > 0.5, choice, NEG_INF)
    idxs, ws = [], []
    for _ in range(TOP_K):
        _, idx = first_argmax(x)
        hit = lane_f == idx
        idxs.append(idx)
        ws.append(jnp.sum(jnp.where(hit, scores, 0.0), axis=-1, keepdims=True))
        x = jnp.where(hit, NEG_INF, x)
    total = functools.reduce(jnp.add, ws)
    idx_tile = jnp.zeros(logits.shape, F32)
    w_tile = jnp.zeros(logits.shape, F32)
    for j in range(TOP_K):
        idx_tile = jnp.where(lane == j, idxs[j], idx_tile)
        w_tile = jnp.where(lane == j, ws[j] / total * ROUTED_SCALE, w_tile)
    return idx_tile.astype(jnp.int32), w_tile


def _norm_router_kernel(x_ref, g_ref, sc_ref, sh_ref, wh_ref, wl_ref, br_ref,
                        h_ref, htok_ref, idx_ref, wgt_ref, *, n_experts):
    y = _rmsnorm_rows(x_ref[...], g_ref[...])
    h = y * (1.0 + sc_ref[...]) + sh_ref[...]
    h_hi = h.astype(BF16)
    h_ref[...] = h_hi
    _store_token_major(htok_ref, _pack_words(h))
    h_lo = (h - h_hi.astype(F32)).astype(BF16)
    wh = wh_ref[...]
    logits = _dot(h_hi, wh) + _dot(h_lo, wh) + _dot(h_hi, wl_ref[...])
    idx_ref[...], wgt_ref[...] = _route(logits, br_ref[...], n_experts)


def norm_router(rows, x, g, sc, sh, w_router, b_router, row0, tm=256):
    n, d = x.shape
    e = w_router.shape[1]
    tok_rows = d // 2 // LANES
    t0 = row0 // tm
    wr = jnp.zeros((d, LANES), F32).at[:, :e].set(w_router)
    wr_hi = wr.astype(BF16)
    wr_lo = (wr - wr_hi.astype(F32)).astype(BF16)
    br = jnp.zeros((1, LANES), F32).at[0, :e].set(b_router)
    mod_map = lambda i: (rows.mod_index((i + t0) * tm), 0, 0)
    const = lambda i: (0, 0)
    row_map = lambda i: (i + t0, 0)
    return pl.pallas_call(
        functools.partial(_norm_router_kernel, n_experts=e),
        grid=((n - row0) // tm,),
        in_specs=[pl.BlockSpec((tm, d), row_map),
                  pl.BlockSpec((1, d), const),
                  pl.BlockSpec((None, 1, d), mod_map),
                  pl.BlockSpec((None, 1, d), mod_map),
                  pl.BlockSpec((d, LANES), const),
                  pl.BlockSpec((d, LANES), const),
                  pl.BlockSpec((1, LANES), const)],
        out_specs=[pl.BlockSpec((tm, d), row_map),
                   pl.BlockSpec((tm * tok_rows, LANES), row_map),
                   pl.BlockSpec((tm, LANES), row_map),
                   pl.BlockSpec((tm, LANES), row_map)],
        out_shape=[jax.ShapeDtypeStruct((n, d), BF16),
                   jax.ShapeDtypeStruct((n * tok_rows, LANES), U32),
                   jax.ShapeDtypeStruct((n, LANES), jnp.int32),
                   jax.ShapeDtypeStruct((n, LANES), F32)],
        compiler_params=_params("arbitrary"),
        name="norm_router",
    )(x, g.reshape(1, d), sc, sh, wr_hi, wr_lo, br)


def moe_plan(idx, wgt, tok0, n_experts, n_rows, tok_rows):
    t, p = idx.shape[0], idx.shape[0] * TOP_K
    n_tiles = p // MOE_TM + n_experts
    assert n_tiles % 2 == 0 and p % MOE_TM == 0 and 2 * MOE_TM <= n_rows
    flat_e = idx.reshape(p)
    order = jnp.argsort(flat_e, stable=True).astype(jnp.int32)
    counts = jnp.sum((flat_e[:, None] == jnp.arange(n_experts, dtype=jnp.int32)[None, :]).astype(jnp.int32), axis=0)
    tiles_per = (counts + MOE_TM - 1) // MOE_TM
    tile_end = jnp.cumsum(tiles_per)
    tile_start = tile_end - tiles_per
    pair_start = jnp.cumsum(counts) - counts
    tile = jnp.arange(n_tiles, dtype=jnp.int32)
    te = jnp.minimum(jnp.searchsorted(tile_end, tile, side="right"), n_experts - 1).astype(jnp.int32)
    lane = jnp.arange(MOE_TM, dtype=jnp.int32)[None, :]
    r = (tile - tile_start[te])[:, None] * MOE_TM + lane
    valid = (r < counts[te][:, None]) & (tile < tile_end[-1])[:, None]
    pair = order[jnp.clip(pair_start[te][:, None] + r, 0, p - 1)]
    tok = pair // TOP_K + tok0
    plane = pair % TOP_K
    slot = tile[:, None] * MOE_TM + lane
    src = jnp.where(valid, tok, tok0 + slot % t)
    dump = TOP_K * n_rows + (tile % 2)[:, None] * MOE_TM + lane
    dst = jnp.where(valid, plane * n_rows + tok, dump)
    wrow = jnp.where(valid, wgt.reshape(p)[pair], 0.0).reshape(n_tiles * MOE_TM, 1)
    src, dst = src * tok_rows, dst * tok_rows
    blk = jnp.stack([src[0::2], dst[0::2], src[1::2], dst[1::2]], axis=1)
    blk = jnp.concatenate([blk, jnp.zeros_like(blk)], axis=1).astype(jnp.int32)
    return te, blk, wrow


IDX_ROWS = 8


def _stage_pitch(tok_rows):
    return tok_rows if (tok_rows // 8) % 2 == 1 else tok_rows + 8


def _moe_expert_kernel(te_ref, idx_hbm, htok_hbm, wrow_ref,
                       wga_ref, wua_ref, wda_ref, wgb_ref, wub_ref, wdb_ref,
                       yt_hbm,
                       idx_s, in_a, in_b, out_a, out_b, idx_sem, in_sem, out_sem,
                       *, n_steps, n_rows, tok_rows):
    del te_ref
    k = pl.program_id(0)
    par = k % 2
    nxt = 1 - par
    next_step = jnp.where(k + 1 < n_steps, k + 1, 0)
    tm, nch = MOE_TM, tok_rows
    pitch = _stage_pitch(nch)
    stage_in, stage_out = (in_a, in_b), (out_a, out_b)
    dump0 = TOP_K * n_rows * nch

    def idx_copy(step, parity):
        return pltpu.make_async_copy(idx_hbm.at[step], idx_s.at[pl.ds(parity * IDX_ROWS, IDX_ROWS), :], idx_sem)

    def issue_gathers(parity, tile):
        for r in range(tm):
            off = pl.multiple_of(idx_s[parity * IDX_ROWS + 2 * tile, r], nch)
            pltpu.make_async_copy(htok_hbm.at[pl.ds(off, nch), :],
                                  stage_in[tile].at[pl.ds(r * pitch, nch), :], in_sem.at[tile]).start()

    def wait_gathers(tile):
        pltpu.make_async_copy(htok_hbm.at[pl.ds(0, tm * nch), :], stage_in[tile].at[pl.ds(0, tm * nch), :],
                              in_sem.at[tile]).wait()

    def issue_scatters(parity, tile):
        for r in range(tm):
            off = pl.multiple_of(idx_s[parity * IDX_ROWS + 2 * tile + 1, r], nch)
            pltpu.make_async_copy(stage_out[tile].at[pl.ds(r * pitch, nch), :],
                                  yt_hbm.at[pl.ds(off, nch), :], out_sem.at[tile]).start()

    def bulk_scatter(tile):
        return pltpu.make_async_copy(stage_out[tile].at[pl.ds(0, tm * nch), :],
                                     yt_hbm.at[pl.ds(dump0 + tile * tm * nch, tm * nch), :], out_sem.at[tile])

    def compute(tile, wg_ref, wu_ref, wd_ref):
        los, his = [], []
        for c in range(nch):
            lo, hi = _unpack_words(stage_in[tile][pl.ds(c, tm, stride=pitch), :])
            los.append(lo.astype(BF16))
            his.append(hi.astype(BF16))
        x = jnp.concatenate(los + his, axis=1)
        hdim = wg_ref.shape[1]
        wgu = jnp.concatenate([wg_ref[...].astype(BF16), wu_ref[...].astype(BF16)], axis=1)
        gu = _dot(x, wgu)
        g, u = gu[:, :hdim], gu[:, hdim:]
        hid = (g * _sigmoid(g)) * u * wrow_ref[tile * tm:(tile + 1) * tm, :]
        y = _dot(hid.astype(BF16), wd_ref[...].astype(BF16))
        words = _pack_words(y)
        for c in range(nch):
            stage_out[tile][pl.ds(c, tm, stride=pitch), :] = words[:, c * LANES:(c + 1) * LANES]

    @pl.when(k == 0)
    def _():
        first = idx_copy(0, 0)
        first.start()
        first.wait()
        issue_gathers(0, 0)
        issue_gathers(0, 1)
        for tile in range(2):
            stage_out[tile][...] = jnp.zeros(stage_out[tile].shape, U32)
            bulk_scatter(tile).start()

    idx_copy(next_step, nxt).start()

    wait_gathers(0)
    bulk_scatter(0).wait()
    compute(0, wga_ref, wua_ref, wda_ref)
    issue_scatters(par, 0)
    idx_copy(next_step, nxt).wait()
    issue_gathers(nxt, 0)

    wait_gathers(1)
    bulk_scatter(1).wait()
    compute(1, wgb_ref, wub_ref, wdb_ref)
    issue_scatters(par, 1)
    issue_gathers(nxt, 1)

    @pl.when(k == n_steps - 1)
    def _():
        for tile in range(2):
            wait_gathers(tile)
            bulk_scatter(tile).wait()


def moe_experts(htok, te, blk, wrow, wg, wu, wd, l, n_rows):
    n_steps = blk.shape[0]
    tok_rows = htok.shape[0] // n_rows
    _, _, d, hdim = wg.shape
    stage = pltpu.VMEM((MOE_TM * _stage_pitch(tok_rows), LANES), U32)
    w_in_spec = lambda off: pl.BlockSpec((None, None, d, hdim), lambda k, te: (l, te[2 * k + off], 0, 0))
    w_out_spec = lambda off: pl.BlockSpec((None, None, hdim, d), lambda k, te: (l, te[2 * k + off], 0, 0))
    return pl.pallas_call(
        functools.partial(_moe_expert_kernel, n_steps=n_steps, n_rows=n_rows, tok_rows=tok_rows),
        grid_spec=pltpu.PrefetchScalarGridSpec(
            num_scalar_prefetch=1,
            grid=(n_steps,),
            in_specs=[pl.BlockSpec(memory_space=pl.ANY),
                      pl.BlockSpec(memory_space=pl.ANY),
                      pl.BlockSpec((2 * MOE_TM, 1), lambda k, te: (k, 0)),
                      w_in_spec(0), w_in_spec(0), w_out_spec(0),
                      w_in_spec(1), w_in_spec(1), w_out_spec(1)],
            out_specs=pl.BlockSpec(memory_space=pl.ANY),
            scratch_shapes=[pltpu.SMEM((2 * IDX_ROWS, MOE_TM), jnp.int32),
                            stage, stage, stage, stage,
                            pltpu.SemaphoreType.DMA(()),
                            pltpu.SemaphoreType.DMA((2,)),
                            pltpu.SemaphoreType.DMA((2,))],
        ),
        out_shape=jax.ShapeDtypeStruct(((TOP_K + 1) * n_rows * tok_rows, LANES), U32),
        compiler_params=_params("arbitrary"),
        name="moe_experts",
    )(te, blk, htok, wrow, wg, wu, wd, wg, wu, wd)


def _moe_combine_kernel(yt_ref, h_ref, wsg_ref, wsu_ref, wsd_ref, x_ref, g_ref, o_ref, acc_lo, acc_hi, *, tok_rows):
    tm = x_ref.shape[0]
    lo, hi = _unpack_words(yt_ref[0])
    for j in range(1, TOP_K):
        l2, h2 = _unpack_words(yt_ref[j])
        lo, hi = lo + l2, hi + h2
    acc_lo[...] = lo
    acc_hi[...] = hi
    routed = jnp.concatenate(_load_token_major(acc_lo, tm, tok_rows) + _load_token_major(acc_hi, tm, tok_rows), axis=1)
    h = h_ref[...]
    g = _dot(h, wsg_ref[...])
    u = _dot(h, wsu_ref[...])
    shared = _dot(((g * _sigmoid(g)) * u).astype(BF16), wsd_ref[...])
    o_ref[...] = x_ref[...] + g_ref[...] * (routed + shared)


def moe_combine(rows, yt, h, wsg, wsu, wsd, l, x, gate, row0, tm=128):
    n, d = x.shape
    tok_rows = d // 2 // LANES
    yt = yt.reshape(TOP_K + 1, n * tok_rows, LANES)
    s = wsg.shape[2]
    t0 = row0 // tm
    row_map = lambda i: (i + t0, 0)
    return pl.pallas_call(
        functools.partial(_moe_combine_kernel, tok_rows=tok_rows),
        grid=((n - row0) // tm,),
        in_specs=[pl.BlockSpec((TOP_K, tm * tok_rows, LANES), lambda i: (0, i + t0, 0)),
                  pl.BlockSpec((tm, d), row_map),
                  pl.BlockSpec((None, d, s), lambda i: (l, 0, 0)),
                  pl.BlockSpec((None, d, s), lambda i: (l, 0, 0)),
                  pl.BlockSpec((None, s, d), lambda i: (l, 0, 0)),
                  pl.BlockSpec((tm, d), row_map),
                  pl.BlockSpec((None, 1, d), lambda i: (rows.mod_index((i + t0) * tm), 0, 0))],
        out_specs=pl.BlockSpec((tm, d), row_map),
        out_shape=jax.ShapeDtypeStruct(x.shape, x.dtype),
        scratch_shapes=[pltpu.VMEM((tm * tok_rows, LANES), F32), pltpu.VMEM((tm * tok_rows, LANES), F32)],
        input_output_aliases={5: 0},
        compiler_params=_params("arbitrary"),
        name="moe_combine",
    )(yt, h, wsg, wsu, wsd, x, gate)


def kernel(x, c, ctx, c_ctx, g_attn, g_ffn, w_mod_down, w_mod_up, b_mod_up, w_in, q_norm, k_norm, rpb, w_branch_a, w_branch_b, w_out, w_router, b_router, w_exp_gate, w_exp_up, w_exp_down, w_sh_gate, w_sh_up, w_sh_down, g_final):
    batch, seq, d = x.shape
    ctx_len = ctx.shape[1]
    depth = w_in.shape[0]
    n_experts = w_exp_gate.shape[1]
    rows = Rows(batch, ctx_len, seq)
    assert rows.nc % rows.seq == 0 and batch < MOD_ROWS

    a_q, a_kv, b_w = A_HEADS * HEAD_DIM, A_KV_HEADS * HEAD_DIM, B_HEADS * HEAD_DIM
    col_qa = 0
    col_qb = col_qa + a_q
    col_ka = col_qb + b_w
    col_va = col_ka + a_kv
    col_kb = col_va + a_kv
    col_vb = col_kb + b_w
    col_ga = col_vb + b_w
    col_gb = col_ga + d

    cvec = jnp.zeros((MOD_ROWS, d), F32).at[:batch].set(c).at[batch].set(c_ctx)
    mod = modulation_all(cvec, w_mod_down, w_mod_up, b_mod_up)

    w_in_b, w_a_b, w_b_b, w_out_b = (w.astype(BF16) for w in (w_in, w_branch_a, w_branch_b, w_out))
    wsg_b, wsu_b, wsd_b = (w.astype(BF16) for w in (w_sh_gate, w_sh_up, w_sh_down))

    prep_tm = 256
    cos, sin = rope_tables(rows, prep_tm)
    xs = jnp.concatenate([ctx.reshape(rows.nc, d), x.reshape(rows.nx, d)], axis=0)

    for l in range(depth):
        update_ctx = l < depth - 1
        row0 = 0 if update_ctx else rows.nc
        m = [mod[l, :batch + 1, k * d:(k + 1) * d][:, None, :] for k in range(N_MOD)]
        sh1, sc1, gt1, sh2, sc2, gt2 = m

        h1 = norm_mod(rows, xs, g_attn[l], sc1, sh1, 0)
        px = matmul(h1, w_in_b, l)
        qa, ka = qk_prep(rows, px, (col_qa, a_q), (col_ka, a_kv), q_norm[l], k_norm[l], cos, sin, prep_tm)
        oa = attn_a(rows, qa, ka, px, col_va)
        ob = attn_b(rows, px, col_qb, col_kb, col_vb, nb_bias(rpb[l], seq // GRID_W))
        if update_ctx:
            oa = attn_ctx(rows, qa, 0, ka, 0, px, col_va, oa, A_HEADS, A_KV_HEADS)
            ob = attn_ctx(rows, px, col_qb, px, col_kb, px, col_vb, ob, B_HEADS, B_HEADS)
        y = merge(oa, ob, w_a_b, w_b_b, l, px, col_ga, col_gb, row0)
        xs = resid_matmul(rows, y, w_out_b, l, xs, gt1, row0, tm=1024)

        h2, htok, idx_t, wgt_t = norm_router(rows, xs, g_ffn[l], sc2, sh2, w_router[l], b_router[l], row0)
        te, blk, wrow = moe_plan(idx_t[row0:, :TOP_K], wgt_t[row0:, :TOP_K], row0, n_experts, rows.n, d // 2 // LANES)
        yt = moe_experts(htok, te, blk, wrow, w_exp_gate, w_exp_up, w_exp_down, l, rows.n)
        xs = moe_combine(rows, yt, h2, wsg_b, wsu_b, wsd_b, l, xs, gt2, row0)

    return final_norm(xs, g_final, rows.nc).reshape(batch, seq, d)
```

```python
import functools
import math

import jax
import jax.numpy as jnp
from jax import lax
from jax.experimental import pallas as pl
from jax.experimental.pallas import tpu as pltpu

GRID_W = 64
HEAD_DIM = 128
A_HEADS = 16
A_KV_HEADS = 4
B_HEADS = 16
NA_KH_MAX = 8
NA_KW = 16
ROPE_THETA = 10000.0
N_MOD = 6
TOP_K = 8
N_GROUPS = 8
TOPK_GROUPS = 4
ROUTED_SCALE = 2.5
EPS = 1e-6

F32 = jnp.float32
BF16 = jnp.bfloat16
U32 = jnp.uint32
HIGHEST = lax.Precision.HIGHEST
NEG_INF = float("-inf")
LANES = 128

VMEM_LIMIT_BYTES = 56 * 2**20
MOD_ROWS = 16
MOE_TM = 256


def _params(*sem):
    return pltpu.CompilerParams(dimension_semantics=sem, vmem_limit_bytes=VMEM_LIMIT_BYTES)


def _dot(a, b):
    return jnp.dot(a, b, preferred_element_type=F32)


def _dot_nt(a, b):
    return lax.dot_general(a, b, (((1,), (1,)), ((), ())), preferred_element_type=F32)


def _sigmoid(x):
    return 1.0 / (1.0 + jnp.exp(-x))


def _mod_down_kernel(c_ref, w_ref, o_ref):
    c = c_ref[...]
    o_ref[...] = jnp.dot(c * _sigmoid(c), w_ref[...], preferred_element_type=F32, precision=HIGHEST)


def _mod_up_kernel(t_ref, w_ref, b_ref, o_ref):
    o_ref[...] = jnp.dot(t_ref[...], w_ref[...], preferred_element_type=F32, precision=HIGHEST) + b_ref[...]


def modulation_all(cvec, w_down, w_up, b_up, tn=2048):
    depth, d, r = w_down.shape
    n = w_up.shape[2]
    t = pl.pallas_call(
        _mod_down_kernel,
        grid=(depth,),
        in_specs=[pl.BlockSpec((MOD_ROWS, d), lambda l: (0, 0)),
                  pl.BlockSpec((None, d, r), lambda l: (l, 0, 0))],
        out_specs=pl.BlockSpec((None, MOD_ROWS, r), lambda l: (l, 0, 0)),
        out_shape=jax.ShapeDtypeStruct((depth, MOD_ROWS, r), F32),
        compiler_params=_params("arbitrary"),
        name="mod_down",
    )(cvec, w_down)
    return pl.pallas_call(
        _mod_up_kernel,
        grid=(depth, n // tn),
        in_specs=[pl.BlockSpec((None, MOD_ROWS, r), lambda l, j: (l, 0, 0)),
                  pl.BlockSpec((None, r, tn), lambda l, j: (l, 0, j)),
                  pl.BlockSpec((None, 1, tn), lambda l, j: (l, 0, j))],
        out_specs=pl.BlockSpec((None, MOD_ROWS, tn), lambda l, j: (l, 0, j)),
        out_shape=jax.ShapeDtypeStruct((depth, MOD_ROWS, n), F32),
        compiler_params=_params("arbitrary", "arbitrary"),
        name="mod_up",
    )(t, w_up, b_up.reshape(depth, 1, n))


class Rows:
    def __init__(self, batch, ctx_len, seq):
        self.batch, self.ctx_len, self.seq = batch, ctx_len, seq
        self.nc = batch * ctx_len
        self.nx = batch * seq
        self.n = self.nc + self.nx

    def mod_index(self, row):
        return jnp.where(row < self.nc, self.batch, (row - self.nc) // self.seq)


def _rmsnorm_rows(x, g):
    ms = jnp.mean(x * x, axis=-1, keepdims=True)
    return x * lax.rsqrt(ms + EPS) * g


def _norm_mod_kernel(x_ref, g_ref, sc_ref, sh_ref, o_ref):
    y = _rmsnorm_rows(x_ref[...], g_ref[...])
    o_ref[...] = (y * (1.0 + sc_ref[...]) + sh_ref[...]).astype(o_ref.dtype)


def norm_mod(rows, x, g, sc, sh, row0, tm=256):
    n, d = x.shape
    t0 = row0 // tm
    mod_map = lambda i: (rows.mod_index((i + t0) * tm), 0, 0)
    return pl.pallas_call(
        _norm_mod_kernel,
        grid=((n - row0) // tm,),
        in_specs=[pl.BlockSpec((tm, d), lambda i: (i + t0, 0)),
                  pl.BlockSpec((1, d), lambda i: (0, 0)),
                  pl.BlockSpec((None, 1, d), mod_map),
                  pl.BlockSpec((None, 1, d), mod_map)],
        out_specs=pl.BlockSpec((tm, d), lambda i: (i + t0, 0)),
        out_shape=jax.ShapeDtypeStruct((n, d), BF16),
        compiler_params=_params("arbitrary"),
        name="norm_mod",
    )(x, g.reshape(1, d), sc, sh)


def _final_norm_kernel(x_ref, g_ref, o_ref):
    o_ref[...] = _rmsnorm_rows(x_ref[...], g_ref[...])


def final_norm(x, g, row0, tm=256):
    n, d = x.shape
    t0 = row0 // tm
    return pl.pallas_call(
        _final_norm_kernel,
        grid=((n - row0) // tm,),
        in_specs=[pl.BlockSpec((tm, d), lambda i: (i + t0, 0)),
                  pl.BlockSpec((1, d), lambda i: (0, 0))],
        out_specs=pl.BlockSpec((tm, d), lambda i: (i, 0)),
        out_shape=jax.ShapeDtypeStruct((n - row0, d), F32),
        compiler_params=_params("arbitrary"),
        name="final_norm",
    )(x, g.reshape(1, d))


def _mm_kernel(x_ref, w_ref, o_ref):
    o_ref[...] = _dot(x_ref[...], w_ref[...]).astype(o_ref.dtype)


def matmul(x, w, l, tm=1024, tn=1024):
    m, k = x.shape
    n = w.shape[2]
    return pl.pallas_call(
        _mm_kernel,
        grid=(m // tm, n // tn),
        in_specs=[pl.BlockSpec((tm, k), lambda i, j: (i, 0)),
                  pl.BlockSpec((None, k, tn), lambda i, j: (l, 0, j))],
        out_specs=pl.BlockSpec((tm, tn), lambda i, j: (i, j)),
        out_shape=jax.ShapeDtypeStruct((m, n), BF16),
        compiler_params=_params("arbitrary", "arbitrary"),
        name="matmul",
    )(x, w)


def rope_tables(rows, tm):
    half = HEAD_DIM // 2
    quarter = half // 2
    pos = jnp.arange(rows.seq)
    inv_freq = ROPE_THETA ** (-jnp.arange(quarter, dtype=F32) / quarter)
    ang_r = (pos // GRID_W).astype(F32)[:, None] * inv_freq[None, :]
    ang_c = (pos % GRID_W).astype(F32)[:, None] * inv_freq[None, :]
    ang = jnp.concatenate([ang_r, ang_r, ang_c, ang_c], axis=-1)
    sign = jnp.concatenate([-jnp.ones(quarter), jnp.ones(quarter)] * 2).astype(F32)
    cos = jnp.concatenate([jnp.ones((tm, HEAD_DIM), F32), jnp.cos(ang)], axis=0)
    sin = jnp.concatenate([jnp.zeros((tm, HEAD_DIM), F32), jnp.sin(ang) * sign[None, :]], axis=0)
    return cos, sin


def _qk_prep_kernel(q_ref, k_ref, cos_ref, sin_ref, qg_ref, kg_ref, qo_ref, ko_ref):
    cos = cos_ref[...]
    sin = sin_ref[...]
    quarter = HEAD_DIM // 4
    lane = lax.broadcasted_iota(jnp.int32, cos.shape, 1)
    first = (lane & quarter) == 0

    def prep(x_ref, g, o_ref):
        for h in range(x_ref.shape[1] // HEAD_DIM):
            sl = slice(h * HEAD_DIM, (h + 1) * HEAD_DIM)
            y = _rmsnorm_rows(x_ref[:, sl].astype(F32), g)
            partner = jnp.where(first, pltpu.roll(y, HEAD_DIM - quarter, 1), pltpu.roll(y, quarter, 1))
            o_ref[:, sl] = (y * cos + partner * sin).astype(o_ref.dtype)

    prep(q_ref, qg_ref[...], qo_ref)
    prep(k_ref, kg_ref[...], ko_ref)


def qk_prep(rows, px, q_cols, k_cols, q_norm, k_norm, cos, sin, tm=256):
    n = px.shape[0]
    (q0, qw), (k0, kw) = q_cols, k_cols
    per_seq = rows.seq // tm

    def tab_map(i):
        r = i * tm
        return (jnp.where(r < rows.nc, 0, 1 + ((r - rows.nc) // tm) % per_seq), 0)

    return pl.pallas_call(
        _qk_prep_kernel,
        grid=(n // tm,),
        in_specs=[pl.BlockSpec((tm, qw), lambda i: (i, q0 // qw)),
                  pl.BlockSpec((tm, kw), lambda i: (i, k0 // kw)),
                  pl.BlockSpec((tm, HEAD_DIM), tab_map),
                  pl.BlockSpec((tm, HEAD_DIM), tab_map),
                  pl.BlockSpec((1, HEAD_DIM), lambda i: (0, 0)),
                  pl.BlockSpec((1, HEAD_DIM), lambda i: (0, 0))],
        out_specs=[pl.BlockSpec((tm, qw), lambda i: (i, 0)),
                   pl.BlockSpec((tm, kw), lambda i: (i, 0))],
        out_shape=[jax.ShapeDtypeStruct((n, qw), BF16), jax.ShapeDtypeStruct((n, kw), BF16)],
        compiler_params=_params("arbitrary"),
        name="qk_prep",
    )(px, px, cos, sin, q_norm.reshape(1, HEAD_DIM), k_norm.reshape(1, HEAD_DIM))


def _softmax_pv(scores, values, scale):
    c = scale * math.log2(math.e)
    m = functools.reduce(jnp.maximum, [jnp.max(s, axis=-1, keepdims=True) for s in scores])
    ps = [jnp.exp2((s - m) * c) for s in scores]
    l = functools.reduce(jnp.add, [jnp.sum(p, axis=-1, keepdims=True) for p in ps])
    o = functools.reduce(jnp.add, [_dot(p.astype(v.dtype), v) for p, v in zip(ps, values)])
    return o / l


def _attn_ctx_kernel(q_ref, k_ref, v_ref, prev_ref, o_ref, *, nq, nk, scale):
    del prev_ref
    for h in range(nq):
        kh = h * nk // nq
        ks = slice(kh * HEAD_DIM, (kh + 1) * HEAD_DIM)
        qs = slice(h * HEAD_DIM, (h + 1) * HEAD_DIM)
        s = _dot_nt(q_ref[:, qs], k_ref[:, ks])
        o_ref[:, qs] = _softmax_pv([s], [v_ref[:, ks]], scale).astype(o_ref.dtype)


def attn_ctx(rows, q_arr, q0, k_arr, k0, v_arr, v0, out_prev, n_heads, n_kv_heads, nq=4):
    nk = nq * n_kv_heads // n_heads
    qw, kw = nq * HEAD_DIM, nk * HEAD_DIM
    cl = rows.ctx_len
    return pl.pallas_call(
        functools.partial(_attn_ctx_kernel, nq=nq, nk=nk, scale=HEAD_DIM ** -0.5),
        grid=(rows.batch, n_heads // nq),
        in_specs=[pl.BlockSpec((cl, qw), lambda b, g: (b, q0 // qw + g)),
                  pl.BlockSpec((cl, kw), lambda b, g: (b, k0 // kw + g)),
                  pl.BlockSpec((cl, kw), lambda b, g: (b, v0 // kw + g)),
                  pl.BlockSpec(memory_space=pl.ANY)],
        out_specs=pl.BlockSpec((cl, qw), lambda b, g: (b, g)),
        out_shape=jax.ShapeDtypeStruct(out_prev.shape, out_prev.dtype),
        input_output_aliases={3: 0},
        compiler_params=_params("arbitrary", "arbitrary"),
        name="attn_ctx",
    )(q_arr, k_arr, v_arr, out_prev)


def _attn_a_kernel(q_ref, kc_ref, kl_ref, vc_ref, vl_ref, o_ref, *, nq, scale):
    kc, kl, vc, vl = kc_ref[...], kl_ref[...], vc_ref[...], vl_ref[...]
    for h in range(nq):
        qs = slice(h * HEAD_DIM, (h + 1) * HEAD_DIM)
        q = q_ref[:, qs]
        o = _softmax_pv([_dot_nt(q, kc), _dot_nt(q, kl)], [vc, vl], scale)
        o_ref[:, qs] = o.astype(o_ref.dtype)


def attn_a(rows, qa, ka, px, v0, tq=512):
    nq = A_HEADS // A_KV_HEADS
    qw = nq * HEAD_DIM
    n_q_blocks = rows.seq // tq
    lat0 = rows.nc // rows.seq
    return pl.pallas_call(
        functools.partial(_attn_a_kernel, nq=nq, scale=HEAD_DIM ** -0.5),
        grid=(rows.batch, A_KV_HEADS, n_q_blocks),
        in_specs=[pl.BlockSpec((tq, qw), lambda b, g, i: (rows.nc // tq + b * n_q_blocks + i, g)),
                  pl.BlockSpec((rows.ctx_len, HEAD_DIM), lambda b, g, i: (b, g)),
                  pl.BlockSpec((rows.seq, HEAD_DIM), lambda b, g, i: (lat0 + b, g)),
                  pl.BlockSpec((rows.ctx_len, HEAD_DIM), lambda b, g, i: (b, v0 // HEAD_DIM + g)),
                  pl.BlockSpec((rows.seq, HEAD_DIM), lambda b, g, i: (lat0 + b, v0 // HEAD_DIM + g))],
        out_specs=pl.BlockSpec((tq, qw), lambda b, g, i: (rows.nc // tq + b * n_q_blocks + i, g)),
        out_shape=jax.ShapeDtypeStruct((rows.n, A_HEADS * HEAD_DIM), BF16),
        compiler_params=_params("arbitrary", "arbitrary", "arbitrary"),
        name="attn_a",
    )(qa, ka, ka, px, px)


NB_QROWS = 8
NB_KROWS = 16


def _nb_slab_start(i, grid_rows):
    lo = i * NB_QROWS - NA_KH_MAX // 2
    hi = grid_rows - NB_KROWS
    if isinstance(i, int):
        return min(max(lo, 0), hi)
    return jnp.clip(lo, 0, hi)


def _nb_bias_kernel(rpb_ref, o_ref, *, grid_rows, inv_scale):
    h = pl.program_id(0)
    w = GRID_W
    rpb_h, rpb_w = 2 * NA_KH_MAX - 1, 2 * NA_KW - 1
    qc = lax.broadcasted_iota(jnp.int32, (w, w), 0)
    kc = lax.broadcasted_iota(jnp.int32, (w, w), 1)
    cs = jnp.clip(qc - NA_KW // 2, 0, w - NA_KW)
    col_ok = (kc >= cs) & (kc < cs + NA_KW)
    dc = kc - qc + NA_KW - 1
    dc_masks = [dc == j for j in range(rpb_w)]
    tiles = []
    for dr in range(rpb_h):
        t = jnp.zeros((w, w), F32)
        for j in range(rpb_w):
            t = jnp.where(dc_masks[j], rpb_ref[(h * rpb_h + dr) * rpb_w + j], t)
        tiles.append(jnp.where(col_ok, t * inv_scale, NEG_INF))
    blank = jnp.full((w, w), NEG_INF, F32)
    for i in range(grid_rows // NB_QROWS):
        k0 = _nb_slab_start(i, grid_rows)
        for qr in range(NB_QROWS):
            r = i * NB_QROWS + qr
            rs = min(max(r - NA_KH_MAX // 2, 0), grid_rows - NA_KH_MAX)
            row = []
            for kr in range(NB_KROWS):
                key_row = k0 + kr
                ok = rs <= key_row < rs + NA_KH_MAX
                row.append(tiles[key_row - r + NA_KH_MAX - 1] if ok else blank)
            o_ref[i, qr * w:(qr + 1) * w, :] = jnp.concatenate(row, axis=1)


def nb_bias(rpb, grid_rows):
    n_heads = rpb.shape[0]
    nqb = grid_rows // NB_QROWS
    tq, tk = NB_QROWS * GRID_W, NB_KROWS * GRID_W
    return pl.pallas_call(
        functools.partial(_nb_bias_kernel, grid_rows=grid_rows, inv_scale=HEAD_DIM ** 0.5),
        grid=(n_heads,),
        in_specs=[pl.BlockSpec(memory_space=pltpu.SMEM)],
        out_specs=pl.BlockSpec((None, nqb, tq, tk), lambda h: (h, 0, 0, 0)),
        out_shape=jax.ShapeDtypeStruct((n_heads, nqb, tq, tk), F32),
        compiler_params=_params("arbitrary"),
        name="nb_bias",
    )(rpb.reshape(-1))


def _attn_b_kernel(q_ref, kc_ref, kl_ref, vc_ref, vl_ref, bias_ref, o_ref, *, nh, scale, grid_rows):
    i = pl.program_id(1)
    tk = NB_KROWS * GRID_W
    start = pl.multiple_of(_nb_slab_start(i, grid_rows) * GRID_W, GRID_W)
    for h in range(nh):
        hs = slice(h * HEAD_DIM, (h + 1) * HEAD_DIM)
        q = q_ref[:, hs]
        s_c = _dot_nt(q, kc_ref[:, hs])
        s_w = _dot_nt(q, kl_ref[pl.ds(start, tk), hs]) + bias_ref[h]
        o = _softmax_pv([s_c, s_w], [vc_ref[:, hs], vl_ref[pl.ds(start, tk), hs]], scale)
        o_ref[:, hs] = o.astype(o_ref.dtype)


def attn_b(rows, px, q0, k0, v0, bias, nh=2):
    grid_rows = rows.seq // GRID_W
    tq = NB_QROWS * GRID_W
    nqb = rows.seq // tq
    hw = nh * HEAD_DIM
    lat0 = rows.nc // rows.seq
    q_map = lambda g, i, b: (rows.nc // tq + b * nqb + i, q0 // hw + g)
    return pl.pallas_call(
        functools.partial(_attn_b_kernel, nh=nh, scale=HEAD_DIM ** -0.5, grid_rows=grid_rows),
        grid=(B_HEADS // nh, nqb, rows.batch),
        in_specs=[pl.BlockSpec((tq, hw), q_map),
                  pl.BlockSpec((rows.ctx_len, hw), lambda g, i, b: (b, k0 // hw + g)),
                  pl.BlockSpec((rows.seq, hw), lambda g, i, b: (lat0 + b, k0 // hw + g)),
                  pl.BlockSpec((rows.ctx_len, hw), lambda g, i, b: (b, v0 // hw + g)),
                  pl.BlockSpec((rows.seq, hw), lambda g, i, b: (lat0 + b, v0 // hw + g)),
                  pl.BlockSpec((nh, None, tq, NB_KROWS * GRID_W), lambda g, i, b: (g, i, 0, 0))],
        out_specs=pl.BlockSpec((tq, hw), lambda g, i, b: (rows.nc // tq + b * nqb + i, g)),
        out_shape=jax.ShapeDtypeStruct((rows.n, B_HEADS * HEAD_DIM), BF16),
        compiler_params=_params("arbitrary", "arbitrary", "arbitrary"),
        name="attn_b",
    )(px, px, px, px, px, bias)


def _merge_kernel(oa_ref, ob_ref, wa_ref, wb_ref, ga_ref, gb_ref, o_ref):
    a = _dot(oa_ref[...], wa_ref[...])
    b = _dot(ob_ref[...], wb_ref[...])
    y = _sigmoid(ga_ref[...].astype(F32)) * a + _sigmoid(gb_ref[...].astype(F32)) * b
    o_ref[...] = y.astype(o_ref.dtype)


def merge(oa, ob, wa, wb, l, px, ga0, gb0, row0, tm=1024, tn=1024):
    n, k = oa.shape
    d = wa.shape[2]
    t0 = row0 // tm
    return pl.pallas_call(
        _merge_kernel,
        grid=((n - row0) // tm, d // tn),
        in_specs=[pl.BlockSpec((tm, k), lambda i, j: (i + t0, 0)),
                  pl.BlockSpec((tm, k), lambda i, j: (i + t0, 0)),
                  pl.BlockSpec((None, k, tn), lambda i, j: (l, 0, j)),
                  pl.BlockSpec((None, k, tn), lambda i, j: (l, 0, j)),
                  pl.BlockSpec((tm, tn), lambda i, j: (i + t0, ga0 // tn + j)),
                  pl.BlockSpec((tm, tn), lambda i, j: (i + t0, gb0 // tn + j))],
        out_specs=pl.BlockSpec((tm, tn), lambda i, j: (i + t0, j)),
        out_shape=jax.ShapeDtypeStruct((n, d), BF16),
        compiler_params=_params("arbitrary", "arbitrary"),
        name="merge",
    )(oa, ob, wa, wb, px, px)


def _resid_mm_kernel(y_ref, w_ref, x_ref, g_ref, o_ref):
    o_ref[...] = x_ref[...] + g_ref[...] * _dot(y_ref[...], w_ref[...])


def resid_matmul(rows, y, w, l, x, gate, row0, tm, tn=1024):
    n, k = y.shape
    d = w.shape[2]
    t0 = row0 // tm
    return pl.pallas_call(
        _resid_mm_kernel,
        grid=((n - row0) // tm, d // tn),
        in_specs=[pl.BlockSpec((tm, k), lambda i, j: (i + t0, 0)),
                  pl.BlockSpec((None, k, tn), lambda i, j: (l, 0, j)),
                  pl.BlockSpec((tm, tn), lambda i, j: (i + t0, j)),
                  pl.BlockSpec((None, 1, tn), lambda i, j: (rows.mod_index((i + t0) * tm), 0, j))],
        out_specs=pl.BlockSpec((tm, tn), lambda i, j: (i + t0, j)),
        out_shape=jax.ShapeDtypeStruct(x.shape, x.dtype),
        input_output_aliases={2: 0},
        compiler_params=_params("arbitrary", "arbitrary"),
        name="resid_matmul",
    )(y, w, x, gate)


def _pack_words(v):
    half = v.shape[1] // 2
    bits = lax.bitcast_convert_type(v.astype(BF16).astype(F32), U32)
    return (bits[:, :half] >> 16) | bits[:, half:]


def _unpack_words(w):
    lo = lax.bitcast_convert_type(w << 16, F32)
    hi = lax.bitcast_convert_type(w & U32(0xFFFF0000), F32)
    return lo, hi


def _store_token_major(ref, words):
    rows, nch = words.shape[0], words.shape[1] // LANES
    for c in range(nch):
        ref[pl.ds(c, rows, stride=nch), :] = words[:, c * LANES:(c + 1) * LANES]


def _load_token_major(ref, rows, nch):
    return [ref[pl.ds(c, rows, stride=nch), :] for c in range(nch)]


def _route(logits, bias, n_experts):
    lane = lax.broadcasted_iota(jnp.int32, logits.shape, 1)
    lane_f = lane.astype(F32)
    per_group = n_experts // N_GROUPS
    valid = lane < n_experts
    scores = _sigmoid(logits)
    choice = jnp.where(valid, scores + bias, NEG_INF)
    big = float(LANES)

    def first_argmax(x):
        m = jnp.max(x, axis=-1, keepdims=True)
        idx = jnp.min(jnp.where(x == m, lane_f, big), axis=-1, keepdims=True)
        return m, idx

    in_group = [(lane >= g * per_group) & (lane < (g + 1) * per_group) for g in range(N_GROUPS)]
    group_score = []
    for g in range(N_GROUPS):
        xg = jnp.where(in_group[g], choice, NEG_INF)
        m1, i1 = first_argmax(xg)
        m2 = jnp.max(jnp.where(lane_f == i1, NEG_INF, xg), axis=-1, keepdims=True)
        group_score.append(m1 + m2)
    ok = jnp.zeros(logits.shape, F32)
    for g in range(N_GROUPS):
        ahead = jnp.zeros(group_score[g].shape, F32)
        for o in range(N_GROUPS):
            if o != g:
                wins = group_score[o] >= group_score[g] if o < g else group_score[o] > group_score[g]
                ahead = ahead + jnp.where(wins, 1.0, 0.0)
        ok = jnp.where(in_group[g], jnp.where(ahead < TOPK_GROUPS, 1.0, 0.0), ok)
    x = jnp.where(ok > 0.5, choice, NEG_INF)
    idxs, ws = [], []
    for _ in range(TOP_K):
        _, idx = first_argmax(x)
        hit = lane_f == idx
        idxs.append(idx)
        ws.append(jnp.sum(jnp.where(hit, scores, 0.0), axis=-1, keepdims=True))
        x = jnp.where(hit, NEG_INF, x)
    total = functools.reduce(jnp.add, ws)
    idx_tile = jnp.zeros(logits.shape, F32)
    w_tile = jnp.zeros(logits.shape, F32)
    for j in range(TOP_K):
        idx_tile = jnp.where(lane == j, idxs[j], idx_tile)
        w_tile = jnp.where(lane == j, ws[j] / total * ROUTED_SCALE, w_tile)
    return idx_tile.astype(jnp.int32), w_tile


def _norm_router_kernel(x_ref, g_ref, sc_ref, sh_ref, wh_ref, wl_ref, br_ref,
                        h_ref, htok_ref, idx_ref, wgt_ref, *, n_experts):
    y = _rmsnorm_rows(x_ref[...], g_ref[...])
    h = y * (1.0 + sc_ref[...]) + sh_ref[...]
    h_hi = h.astype(BF16)
    h_ref[...] = h_hi
    _store_token_major(htok_ref, _pack_words(h))
    h_lo = (h - h_hi.astype(F32)).astype(BF16)
    wh = wh_ref[...]
    logits = _dot(h_hi, wh) + _dot(h_lo, wh) + _dot(h_hi, wl_ref[...])
    idx_ref[...], wgt_ref[...] = _route(logits, br_ref[...], n_experts)


def norm_router(rows, x, g, sc, sh, w_router, b_router, row0, tm=256):
    n, d = x.shape
    e = w_router.shape[1]
    tok_rows = d // 2 // LANES
    t0 = row0 // tm
    wr = jnp.zeros((d, LANES), F32).at[:, :e].set(w_router)
    wr_hi = wr.astype(BF16)
    wr_lo = (wr - wr_hi.astype(F32)).astype(BF16)
    br = jnp.zeros((1, LANES), F32).at[0, :e].set(b_router)
    mod_map = lambda i: (rows.mod_index((i + t0) * tm), 0, 0)
    const = lambda i: (0, 0)
    row_map = lambda i: (i + t0, 0)
    return pl.pallas_call(
        functools.partial(_norm_router_kernel, n_experts=e),
        grid=((n - row0) // tm,),
        in_specs=[pl.BlockSpec((tm, d), row_map),
                  pl.BlockSpec((1, d), const),
                  pl.BlockSpec((None, 1, d), mod_map),
                  pl.BlockSpec((None, 1, d), mod_map),
                  pl.BlockSpec((d, LANES), const),
                  pl.BlockSpec((d, LANES), const),
                  pl.BlockSpec((1, LANES), const)],
        out_specs=[pl.BlockSpec((tm, d), row_map),
                   pl.BlockSpec((tm * tok_rows, LANES), row_map),
                   pl.BlockSpec((tm, LANES), row_map),
                   pl.BlockSpec((tm, LANES), row_map)],
        out_shape=[jax.ShapeDtypeStruct((n, d), BF16),
                   jax.ShapeDtypeStruct((n * tok_rows, LANES), U32),
                   jax.ShapeDtypeStruct((n, LANES), jnp.int32),
                   jax.ShapeDtypeStruct((n, LANES), F32)],
        compiler_params=_params("arbitrary"),
        name="norm_router",
    )(x, g.reshape(1, d), sc, sh, wr_hi, wr_lo, br)


def moe_plan(idx, wgt, tok0, n_experts, n_rows, tok_rows):
    t, p = idx.shape[0], idx.shape[0] * TOP_K
    n_tiles = p // MOE_TM + n_experts
    assert n_tiles % 2 == 0 and p % MOE_TM == 0 and 2 * MOE_TM <= n_rows
    flat_e = idx.reshape(p)
    order = jnp.argsort(flat_e, stable=True).astype(jnp.int32)
    counts = jnp.sum((flat_e[:, None] == jnp.arange(n_experts, dtype=jnp.int32)[None, :]).astype(jnp.int32), axis=0)
    tiles_per = (counts + MOE_TM - 1) // MOE_TM
    tile_end = jnp.cumsum(tiles_per)
    tile_start = tile_end - tiles_per
    pair_start = jnp.cumsum(counts) - counts
    tile = jnp.arange(n_tiles, dtype=jnp.int32)
    te = jnp.sum((tile[:, None] >= tile_end[None, :]).astype(jnp.int32), axis=1)
    te = jnp.minimum(te, n_experts - 1)
    lane = jnp.arange(MOE_TM, dtype=jnp.int32)[None, :]
    r = (tile - tile_start[te])[:, None] * MOE_TM + lane
    valid = (r < counts[te][:, None]) & (tile < tile_end[-1])[:, None]
    pair = order[jnp.clip(pair_start[te][:, None] + r, 0, p - 1)]
    tok = pair // TOP_K + tok0
    plane = pair % TOP_K
    slot = tile[:, None] * MOE_TM + lane
    src = jnp.where(valid, tok, tok0 + slot % t)
    dump = TOP_K * n_rows + (tile % 2)[:, None] * MOE_TM + lane
    dst = jnp.where(valid, plane * n_rows + tok, dump)
    wrow = jnp.where(valid, wgt.reshape(p)[pair], 0.0).reshape(n_tiles * MOE_TM, 1)
    src, dst = src * tok_rows, dst * tok_rows
    blk = jnp.stack([src[0::2], dst[0::2], src[1::2], dst[1::2]], axis=1)
    blk = jnp.concatenate([blk, jnp.zeros_like(blk)], axis=1).astype(jnp.int32)
    used_steps = ((tile_end[-1] + 1) // 2).astype(jnp.int32).reshape(1)
    return te, used_steps, blk, wrow


IDX_ROWS = 8


def _stage_pitch(tok_rows):
    return tok_rows if (tok_rows // 8) % 2 == 1 else tok_rows + 8


def _moe_expert_kernel(te_ref, ns_ref, idx_hbm, htok_hbm, wrow_ref,
                       wga_ref, wua_ref, wda_ref, wgb_ref, wub_ref, wdb_ref,
                       yt_hbm,
                       idx_s, in_a, in_b, out_a, out_b, idx_sem, in_sem, out_sem,
                       *, n_rows, tok_rows):
    del te_ref
    n_steps = ns_ref[0]
    k = pl.program_id(0)
    par = k % 2
    nxt = 1 - par
    next_step = jnp.where(k + 1 < n_steps, k + 1, 0)
    tm, nch = MOE_TM, tok_rows
    pitch = _stage_pitch(nch)
    stage_in, stage_out = (in_a, in_b), (out_a, out_b)
    dump0 = TOP_K * n_rows * nch

    def idx_copy(step, parity):
        return pltpu.make_async_copy(idx_hbm.at[step], idx_s.at[pl.ds(parity * IDX_ROWS, IDX_ROWS), :], idx_sem)

    def issue_gathers(parity, tile):
        for r in range(tm):
            off = pl.multiple_of(idx_s[parity * IDX_ROWS + 2 * tile, r], nch)
            pltpu.make_async_copy(htok_hbm.at[pl.ds(off, nch), :],
                                  stage_in[tile].at[pl.ds(r * pitch, nch), :], in_sem.at[tile]).start()

    def wait_gathers(tile):
        pltpu.make_async_copy(htok_hbm.at[pl.ds(0, tm * nch), :], stage_in[tile].at[pl.ds(0, tm * nch), :],
                              in_sem.at[tile]).wait()

    def issue_scatters(parity, tile):
        for r in range(tm):
            off = pl.multiple_of(idx_s[parity * IDX_ROWS + 2 * tile + 1, r], nch)
            pltpu.make_async_copy(stage_out[tile].at[pl.ds(r * pitch, nch), :],
                                  yt_hbm.at[pl.ds(off, nch), :], out_sem.at[tile]).start()

    def bulk_scatter(tile):
        return pltpu.make_async_copy(stage_out[tile].at[pl.ds(0, tm * nch), :],
                                     yt_hbm.at[pl.ds(dump0 + tile * tm * nch, tm * nch), :], out_sem.at[tile])

    def load_rows(tile):
        los, his = [], []
        for c in range(nch):
            lo, hi = _unpack_words(stage_in[tile][pl.ds(c, tm, stride=pitch), :])
            los.append(lo.astype(BF16))
            his.append(hi.astype(BF16))
        return jnp.concatenate(los + his, axis=1)

    def compute(tile, wg_ref, wu_ref, wd_ref):
        x = load_rows(tile)
        hdim = wg_ref.shape[1]
        wgu = jnp.concatenate([wg_ref[...].astype(BF16), wu_ref[...].astype(BF16)], axis=1)
        gu = _dot(x, wgu)
        g, u = gu[:, :hdim], gu[:, hdim:]
        hid = (g * _sigmoid(g)) * u * wrow_ref[tile * tm:(tile + 1) * tm, :]
        y = _dot(hid.astype(BF16), wd_ref[...].astype(BF16))
        words = _pack_words(y)
        for c in range(nch):
            stage_out[tile][pl.ds(c, tm, stride=pitch), :] = words[:, c * LANES:(c + 1) * LANES]

    @pl.when(k < n_steps)
    def _():
        @pl.when(k == 0)
        def _():
            first = idx_copy(0, 0)
            first.start()
            first.wait()
            issue_gathers(0, 0)
            issue_gathers(0, 1)
            for tile in range(2):
                stage_out[tile][...] = jnp.zeros(stage_out[tile].shape, U32)
                bulk_scatter(tile).start()

        idx_copy(next_step, nxt).start()

        wait_gathers(0)
        bulk_scatter(0).wait()
        compute(0, wga_ref, wua_ref, wda_ref)
        issue_scatters(par, 0)
        idx_copy(next_step, nxt).wait()
        issue_gathers(nxt, 0)

        wait_gathers(1)
        bulk_scatter(1).wait()
        compute(1, wgb_ref, wub_ref, wdb_ref)
        issue_scatters(par, 1)
        issue_gathers(nxt, 1)

        @pl.when(k == n_steps - 1)
        def _():
            for tile in range(2):
                wait_gathers(tile)
                bulk_scatter(tile).wait()


def moe_experts(htok, te, used_steps, blk, wrow, wg, wu, wd, l, n_rows):
    n_steps = blk.shape[0]
    tok_rows = htok.shape[0] // n_rows
    _, _, d, hdim = wg.shape
    stage = pltpu.VMEM((MOE_TM * _stage_pitch(tok_rows), LANES), U32)
    w_in_spec = lambda off: pl.BlockSpec((None, None, d, hdim), lambda k, te, ns: (l, te[2 * k + off], 0, 0))
    w_out_spec = lambda off: pl.BlockSpec((None, None, hdim, d), lambda k, te, ns: (l, te[2 * k + off], 0, 0))
    return pl.pallas_call(
        functools.partial(_moe_expert_kernel, n_rows=n_rows, tok_rows=tok_rows),
        grid_spec=pltpu.PrefetchScalarGridSpec(
            num_scalar_prefetch=2,
            grid=(n_steps,),
            in_specs=[pl.BlockSpec(memory_space=pl.ANY),
                      pl.BlockSpec(memory_space=pl.ANY),
                      pl.BlockSpec((2 * MOE_TM, 1), lambda k, te, ns: (k, 0)),
                      w_in_spec(0), w_in_spec(0), w_out_spec(0),
                      w_in_spec(1), w_in_spec(1), w_out_spec(1)],
            out_specs=pl.BlockSpec(memory_space=pl.ANY),
            scratch_shapes=[pltpu.SMEM((2 * IDX_ROWS, MOE_TM), jnp.int32),
                            stage, stage, stage, stage,
                            pltpu.SemaphoreType.DMA(()),
                            pltpu.SemaphoreType.DMA((2,)),
                            pltpu.SemaphoreType.DMA((2,))],
        ),
        out_shape=jax.ShapeDtypeStruct(((TOP_K + 1) * n_rows * tok_rows, LANES), U32),
        compiler_params=_params("arbitrary"),
        name="moe_experts",
    )(te, used_steps, blk, htok, wrow, wg, wu, wd, wg, wu, wd)


def _moe_combine_kernel(yt_ref, h_ref, wsg_ref, wsu_ref, wsd_ref, x_ref, g_ref, o_ref, acc_lo, acc_hi, *, tok_rows):
    tm = x_ref.shape[0]
    lo, hi = _unpack_words(yt_ref[0])
    for j in range(1, TOP_K):
        l2, h2 = _unpack_words(yt_ref[j])
        lo, hi = lo + l2, hi + h2
    acc_lo[...] = lo
    acc_hi[...] = hi
    routed = jnp.concatenate(_load_token_major(acc_lo, tm, tok_rows) + _load_token_major(acc_hi, tm, tok_rows), axis=1)
    h = h_ref[...]
    g = _dot(h, wsg_ref[...])
    u = _dot(h, wsu_ref[...])
    shared = _dot(((g * _sigmoid(g)) * u).astype(BF16), wsd_ref[...])
    o_ref[...] = x_ref[...] + g_ref[...] * (routed + shared)


def moe_combine(rows, yt, h, wsg, wsu, wsd, l, x, gate, row0, tm=128):
    n, d = x.shape
    tok_rows = d // 2 // LANES
    yt = yt.reshape(TOP_K + 1, n * tok_rows, LANES)
    s = wsg.shape[2]
    t0 = row0 // tm
    row_map = lambda i: (i + t0, 0)
    return pl.pallas_call(
        functools.partial(_moe_combine_kernel, tok_rows=tok_rows),
        grid=((n - row0) // tm,),
        in_specs=[pl.BlockSpec((TOP_K, tm * tok_rows, LANES), lambda i: (0, i + t0, 0)),
                  pl.BlockSpec((tm, d), row_map),
                  pl.BlockSpec((None, d, s), lambda i: (l, 0, 0)),
                  pl.BlockSpec((None, d, s), lambda i: (l, 0, 0)),
                  pl.BlockSpec((None, s, d), lambda i: (l, 0, 0)),
                  pl.BlockSpec((tm, d), row_map),
                  pl.BlockSpec((None, 1, d), lambda i: (rows.mod_index((i + t0) * tm), 0, 0))],
        out_specs=pl.BlockSpec((tm, d), row_map),
        out_shape=jax.ShapeDtypeStruct(x.shape, x.dtype),
        scratch_shapes=[pltpu.VMEM((tm * tok_rows, LANES), F32), pltpu.VMEM((tm * tok_rows, LANES), F32)],
        input_output_aliases={5: 0},
        compiler_params=_params("arbitrary"),
        name="moe_combine",
    )(yt, h, wsg, wsu, wsd, x, gate)


def kernel(x, c, ctx, c_ctx, g_attn, g_ffn, w_mod_down, w_mod_up, b_mod_up, w_in, q_norm, k_norm, rpb, w_branch_a, w_branch_b, w_out, w_router, b_router, w_exp_gate, w_exp_up, w_exp_down, w_sh_gate, w_sh_up, w_sh_down, g_final):
    batch, seq, d = x.shape
    ctx_len = ctx.shape[1]
    depth = w_in.shape[0]
    n_experts = w_exp_gate.shape[1]
    rows = Rows(batch, ctx_len, seq)
    assert rows.nc % rows.seq == 0 and batch < MOD_ROWS

    a_q, a_kv, b_w = A_HEADS * HEAD_DIM, A_KV_HEADS * HEAD_DIM, B_HEADS * HEAD_DIM
    col_qa = 0
    col_qb = col_qa + a_q
    col_ka = col_qb + b_w
    col_va = col_ka + a_kv
    col_kb = col_va + a_kv
    col_vb = col_kb + b_w
    col_ga = col_vb + b_w
    col_gb = col_ga + d

    cvec = jnp.zeros((MOD_ROWS, d), F32).at[:batch].set(c).at[batch].set(c_ctx)
    mod = modulation_all(cvec, w_mod_down, w_mod_up, b_mod_up)

    w_in_b, w_a_b, w_b_b, w_out_b = (w.astype(BF16) for w in (w_in, w_branch_a, w_branch_b, w_out))
    wsg_b, wsu_b, wsd_b = (w.astype(BF16) for w in (w_sh_gate, w_sh_up, w_sh_down))

    prep_tm = 256
    cos, sin = rope_tables(rows, prep_tm)
    xs = jnp.concatenate([ctx.reshape(rows.nc, d), x.reshape(rows.nx, d)], axis=0)

    for l in range(depth):
        update_ctx = l < depth - 1
        row0 = 0 if update_ctx else rows.nc
        m = [mod[l, :batch + 1, k * d:(k + 1) * d][:, None, :] for k in range(N_MOD)]
        sh1, sc1, gt1, sh2, sc2, gt2 = m

        h1 = norm_mod(rows, xs, g_attn[l], sc1, sh1, 0)
        px = matmul(h1, w_in_b, l)
        qa, ka = qk_prep(rows, px, (col_qa, a_q), (col_ka, a_kv), q_norm[l], k_norm[l], cos, sin, prep_tm)
        oa = attn_a(rows, qa, ka, px, col_va)
        ob = attn_b(rows, px, col_qb, col_kb, col_vb, nb_bias(rpb[l], seq // GRID_W))
        if update_ctx:
            oa = attn_ctx(rows, qa, 0, ka, 0, px, col_va, oa, A_HEADS, A_KV_HEADS)
            ob = attn_ctx(rows, px, col_qb, px, col_kb, px, col_vb, ob, B_HEADS, B_HEADS)
        y = merge(oa, ob, w_a_b, w_b_b, l, px, col_ga, col_gb, row0)
        xs = resid_matmul(rows, y, w_out_b, l, xs, gt1, row0, tm=1024)

        h2, htok, idx_t, wgt_t = norm_router(rows, xs, g_ffn[l], sc2, sh2, w_router[l], b_router[l], row0)
        te, used_steps, blk, wrow = moe_plan(idx_t[row0:, :TOP_K], wgt_t[row0:, :TOP_K], row0, n_experts, rows.n, d // 2 // LANES)
        yt = moe_experts(htok, te, used_steps, blk, wrow, w_exp_gate, w_exp_up, w_exp_down, l, rows.n)
        xs = moe_combine(rows, yt, h2, wsg_b, wsu_b, wsd_b, l, xs, gt2, row0)

    return final_norm(xs, g_final, rows.nc).reshape(batch, seq, d)
```

```python
import functools
import math

import jax
import jax.numpy as jnp
from jax import lax
from jax.experimental import pallas as pl
from jax.experimental.pallas import tpu as pltpu

GRID_W = 64
HEAD_DIM = 128
A_HEADS = 16
A_KV_HEADS = 4
B_HEADS = 16
NA_KH_MAX = 8
NA_KW = 16
ROPE_THETA = 10000.0
N_MOD = 6
TOP_K = 8
N_GROUPS = 8
TOPK_GROUPS = 4
ROUTED_SCALE = 2.5
EPS = 1e-6

F32 = jnp.float32
BF16 = jnp.bfloat16
U32 = jnp.uint32
HIGHEST = lax.Precision.HIGHEST
NEG_INF = float("-inf")
LANES = 128

VMEM_LIMIT_BYTES = 56 * 2**20
MOD_ROWS = 16
MOE_TM = 256


def _params(*sem):
    return pltpu.CompilerParams(dimension_semantics=sem, vmem_limit_bytes=VMEM_LIMIT_BYTES)


def _dot(a, b):
    return jnp.dot(a, b, preferred_element_type=F32)


def _dot_nt(a, b):
    return lax.dot_general(a, b, (((1,), (1,)), ((), ())), preferred_element_type=F32)


def _sigmoid(x):
    return 1.0 / (1.0 + jnp.exp(-x))


def _mod_down_kernel(c_ref, w_ref, o_ref):
    c = c_ref[...]
    o_ref[...] = jnp.dot(c * _sigmoid(c), w_ref[...], preferred_element_type=F32, precision=HIGHEST)


def _mod_up_kernel(t_ref, w_ref, b_ref, o_ref):
    o_ref[...] = jnp.dot(t_ref[...], w_ref[...], preferred_element_type=F32, precision=HIGHEST) + b_ref[...]


def modulation_all(cvec, w_down, w_up, b_up, tn=2048):
    depth, d, r = w_down.shape
    n = w_up.shape[2]
    t = pl.pallas_call(
        _mod_down_kernel,
        grid=(depth,),
        in_specs=[pl.BlockSpec((MOD_ROWS, d), lambda l: (0, 0)),
                  pl.BlockSpec((None, d, r), lambda l: (l, 0, 0))],
        out_specs=pl.BlockSpec((None, MOD_ROWS, r), lambda l: (l, 0, 0)),
        out_shape=jax.ShapeDtypeStruct((depth, MOD_ROWS, r), F32),
        compiler_params=_params("arbitrary"),
        name="mod_down",
    )(cvec, w_down)
    return pl.pallas_call(
        _mod_up_kernel,
        grid=(depth, n // tn),
        in_specs=[pl.BlockSpec((None, MOD_ROWS, r), lambda l, j: (l, 0, 0)),
                  pl.BlockSpec((None, r, tn), lambda l, j: (l, 0, j)),
                  pl.BlockSpec((None, 1, tn), lambda l, j: (l, 0, j))],
        out_specs=pl.BlockSpec((None, MOD_ROWS, tn), lambda l, j: (l, 0, j)),
        out_shape=jax.ShapeDtypeStruct((depth, MOD_ROWS, n), F32),
        compiler_params=_params("arbitrary", "arbitrary"),
        name="mod_up",
    )(t, w_up, b_up.reshape(depth, 1, n))


class Rows:
    def __init__(self, batch, ctx_len, seq):
        self.batch, self.ctx_len, self.seq = batch, ctx_len, seq
        self.nc = batch * ctx_len
        self.nx = batch * seq
        self.n = self.nc + self.nx

    def mod_index(self, row):
        return jnp.where(row < self.nc, self.batch, (row - self.nc) // self.seq)


def _rmsnorm_rows(x, g):
    ms = jnp.mean(x * x, axis=-1, keepdims=True)
    return x * lax.rsqrt(ms + EPS) * g


def _norm_mod_kernel(x_ref, g_ref, sc_ref, sh_ref, o_ref):
    y = _rmsnorm_rows(x_ref[...], g_ref[...])
    o_ref[...] = (y * (1.0 + sc_ref[...]) + sh_ref[...]).astype(o_ref.dtype)


def norm_mod(rows, x, g, sc, sh, row0, tm=256):
    n, d = x.shape
    t0 = row0 // tm
    mod_map = lambda i: (rows.mod_index((i + t0) * tm), 0, 0)
    return pl.pallas_call(
        _norm_mod_kernel,
        grid=((n - row0) // tm,),
        in_specs=[pl.BlockSpec((tm, d), lambda i: (i + t0, 0)),
                  pl.BlockSpec((1, d), lambda i: (0, 0)),
                  pl.BlockSpec((None, 1, d), mod_map),
                  pl.BlockSpec((None, 1, d), mod_map)],
        out_specs=pl.BlockSpec((tm, d), lambda i: (i + t0, 0)),
        out_shape=jax.ShapeDtypeStruct((n, d), BF16),
        compiler_params=_params("arbitrary"),
        name="norm_mod",
    )(x, g.reshape(1, d), sc, sh)


def _final_norm_kernel(x_ref, g_ref, o_ref):
    o_ref[...] = _rmsnorm_rows(x_ref[...], g_ref[...])


def final_norm(x, g, row0, tm=256):
    n, d = x.shape
    t0 = row0 // tm
    return pl.pallas_call(
        _final_norm_kernel,
        grid=((n - row0) // tm,),
        in_specs=[pl.BlockSpec((tm, d), lambda i: (i + t0, 0)),
                  pl.BlockSpec((1, d), lambda i: (0, 0))],
        out_specs=pl.BlockSpec((tm, d), lambda i: (i, 0)),
        out_shape=jax.ShapeDtypeStruct((n - row0, d), F32),
        compiler_params=_params("arbitrary"),
        name="final_norm",
    )(x, g.reshape(1, d))


def _mm_kernel(x_ref, w_ref, o_ref):
    o_ref[...] = _dot(x_ref[...], w_ref[...]).astype(o_ref.dtype)


def matmul(x, w, l, tm=1024, tn=1024):
    m, k = x.shape
    n = w.shape[2]
    return pl.pallas_call(
        _mm_kernel,
        grid=(m // tm, n // tn),
        in_specs=[pl.BlockSpec((tm, k), lambda i, j: (i, 0)),
                  pl.BlockSpec((None, k, tn), lambda i, j: (l, 0, j))],
        out_specs=pl.BlockSpec((tm, tn), lambda i, j: (i, j)),
        out_shape=jax.ShapeDtypeStruct((m, n), BF16),
        compiler_params=_params("arbitrary", "arbitrary"),
        name="matmul",
    )(x, w)


def rope_tables(rows, tm):
    half = HEAD_DIM // 2
    quarter = half // 2
    pos = jnp.arange(rows.seq)
    inv_freq = ROPE_THETA ** (-jnp.arange(quarter, dtype=F32) / quarter)
    ang_r = (pos // GRID_W).astype(F32)[:, None] * inv_freq[None, :]
    ang_c = (pos % GRID_W).astype(F32)[:, None] * inv_freq[None, :]
    ang = jnp.concatenate([ang_r, ang_r, ang_c, ang_c], axis=-1)
    sign = jnp.concatenate([-jnp.ones(quarter), jnp.ones(quarter)] * 2).astype(F32)
    cos = jnp.concatenate([jnp.ones((tm, HEAD_DIM), F32), jnp.cos(ang)], axis=0)
    sin = jnp.concatenate([jnp.zeros((tm, HEAD_DIM), F32), jnp.sin(ang) * sign[None, :]], axis=0)
    return cos, sin


def _qk_prep_kernel(q_ref, k_ref, cos_ref, sin_ref, qg_ref, kg_ref, qo_ref, ko_ref):
    cos = cos_ref[...]
    sin = sin_ref[...]
    quarter = HEAD_DIM // 4
    lane = lax.broadcasted_iota(jnp.int32, cos.shape, 1)
    first = (lane & quarter) == 0

    def prep(x_ref, g, o_ref):
        for h in range(x_ref.shape[1] // HEAD_DIM):
            sl = slice(h * HEAD_DIM, (h + 1) * HEAD_DIM)
            y = _rmsnorm_rows(x_ref[:, sl].astype(F32), g)
            partner = jnp.where(first, pltpu.roll(y, HEAD_DIM - quarter, 1), pltpu.roll(y, quarter, 1))
            o_ref[:, sl] = (y * cos + partner * sin).astype(o_ref.dtype)

    prep(q_ref, qg_ref[...], qo_ref)
    prep(k_ref, kg_ref[...], ko_ref)


def qk_prep(rows, px, q_cols, k_cols, q_norm, k_norm, cos, sin, tm=256):
    n = px.shape[0]
    (q0, qw), (k0, kw) = q_cols, k_cols
    per_seq = rows.seq // tm

    def tab_map(i):
        r = i * tm
        return (jnp.where(r < rows.nc, 0, 1 + ((r - rows.nc) // tm) % per_seq), 0)

    return pl.pallas_call(
        _qk_prep_kernel,
        grid=(n // tm,),
        in_specs=[pl.BlockSpec((tm, qw), lambda i: (i, q0 // qw)),
                  pl.BlockSpec((tm, kw), lambda i: (i, k0 // kw)),
                  pl.BlockSpec((tm, HEAD_DIM), tab_map),
                  pl.BlockSpec((tm, HEAD_DIM), tab_map),
                  pl.BlockSpec((1, HEAD_DIM), lambda i: (0, 0)),
                  pl.BlockSpec((1, HEAD_DIM), lambda i: (0, 0))],
        out_specs=[pl.BlockSpec((tm, qw), lambda i: (i, 0)),
                   pl.BlockSpec((tm, kw), lambda i: (i, 0))],
        out_shape=[jax.ShapeDtypeStruct((n, qw), BF16), jax.ShapeDtypeStruct((n, kw), BF16)],
        compiler_params=_params("arbitrary"),
        name="qk_prep",
    )(px, px, cos, sin, q_norm.reshape(1, HEAD_DIM), k_norm.reshape(1, HEAD_DIM))


def _softmax_pv(scores, values, scale):
    c = scale * math.log2(math.e)
    m = functools.reduce(jnp.maximum, [jnp.max(s, axis=-1, keepdims=True) for s in scores])
    ps = [jnp.exp2((s - m) * c) for s in scores]
    l = functools.reduce(jnp.add, [jnp.sum(p, axis=-1, keepdims=True) for p in ps])
    o = functools.reduce(jnp.add, [_dot(p.astype(v.dtype), v) for p, v in zip(ps, values)])
    return o / l


def _attn_ctx_kernel(q_ref, k_ref, v_ref, prev_ref, o_ref, *, nq, nk, scale):
    del prev_ref
    for h in range(nq):
        kh = h * nk // nq
        ks = slice(kh * HEAD_DIM, (kh + 1) * HEAD_DIM)
        qs = slice(h * HEAD_DIM, (h + 1) * HEAD_DIM)
        s = _dot_nt(q_ref[:, qs], k_ref[:, ks])
        o_ref[:, qs] = _softmax_pv([s], [v_ref[:, ks]], scale).astype(o_ref.dtype)


def attn_ctx(rows, q_arr, q0, k_arr, k0, v_arr, v0, out_prev, n_heads, n_kv_heads, nq=4):
    nk = nq * n_kv_heads // n_heads
    qw, kw = nq * HEAD_DIM, nk * HEAD_DIM
    cl = rows.ctx_len
    return pl.pallas_call(
        functools.partial(_attn_ctx_kernel, nq=nq, nk=nk, scale=HEAD_DIM ** -0.5),
        grid=(rows.batch, n_heads // nq),
        in_specs=[pl.BlockSpec((cl, qw), lambda b, g: (b, q0 // qw + g)),
                  pl.BlockSpec((cl, kw), lambda b, g: (b, k0 // kw + g)),
                  pl.BlockSpec((cl, kw), lambda b, g: (b, v0 // kw + g)),
                  pl.BlockSpec(memory_space=pl.ANY)],
        out_specs=pl.BlockSpec((cl, qw), lambda b, g: (b, g)),
        out_shape=jax.ShapeDtypeStruct(out_prev.shape, out_prev.dtype),
        input_output_aliases={3: 0},
        compiler_params=_params("arbitrary", "arbitrary"),
        name="attn_ctx",
    )(q_arr, k_arr, v_arr, out_prev)


def _attn_a_kernel(q_ref, kc_ref, kl_ref, vc_ref, vl_ref, o_ref, *, nq, scale):
    kc, kl, vc, vl = kc_ref[...], kl_ref[...], vc_ref[...], vl_ref[...]
    for h in range(nq):
        qs = slice(h * HEAD_DIM, (h + 1) * HEAD_DIM)
        q = q_ref[:, qs]
        o = _softmax_pv([_dot_nt(q, kc), _dot_nt(q, kl)], [vc, vl], scale)
        o_ref[:, qs] = o.astype(o_ref.dtype)


def attn_a(rows, qa, ka, px, v0, tq=512):
    nq = A_HEADS // A_KV_HEADS
    qw = nq * HEAD_DIM
    n_q_blocks = rows.seq // tq
    lat0 = rows.nc // rows.seq
    return pl.pallas_call(
        functools.partial(_attn_a_kernel, nq=nq, scale=HEAD_DIM ** -0.5),
        grid=(rows.batch, A_KV_HEADS, n_q_blocks),
        in_specs=[pl.BlockSpec((tq, qw), lambda b, g, i: (rows.nc // tq + b * n_q_blocks + i, g)),
                  pl.BlockSpec((rows.ctx_len, HEAD_DIM), lambda b, g, i: (b, g)),
                  pl.BlockSpec((rows.seq, HEAD_DIM), lambda b, g, i: (lat0 + b, g)),
                  pl.BlockSpec((rows.ctx_len, HEAD_DIM), lambda b, g, i: (b, v0 // HEAD_DIM + g)),
                  pl.BlockSpec((rows.seq, HEAD_DIM), lambda b, g, i: (lat0 + b, v0 // HEAD_DIM + g))],
        out_specs=pl.BlockSpec((tq, qw), lambda b, g, i: (rows.nc // tq + b * n_q_blocks + i, g)),
        out_shape=jax.ShapeDtypeStruct((rows.n, A_HEADS * HEAD_DIM), BF16),
        compiler_params=_params("arbitrary", "arbitrary", "arbitrary"),
        name="attn_a",
    )(qa, ka, ka, px, px)


NB_QROWS = 4
NB_KROWS = NB_QROWS + NA_KH_MAX


def _nb_slab_start(i, grid_rows):
    lo = i * NB_QROWS - NA_KH_MAX // 2
    hi = grid_rows - NB_KROWS
    if isinstance(i, int):
        return min(max(lo, 0), hi)
    return jnp.clip(lo, 0, hi)


def _nb_bias_kernel(rpb_ref, o_ref, *, grid_rows, inv_scale):
    h = pl.program_id(0)
    w = GRID_W
    rpb_h, rpb_w = 2 * NA_KH_MAX - 1, 2 * NA_KW - 1
    qc = lax.broadcasted_iota(jnp.int32, (w, w), 0)
    kc = lax.broadcasted_iota(jnp.int32, (w, w), 1)
    cs = jnp.clip(qc - NA_KW // 2, 0, w - NA_KW)
    col_ok = (kc >= cs) & (kc < cs + NA_KW)
    dc = kc - qc + NA_KW - 1
    dc_masks = [dc == j for j in range(rpb_w)]
    tiles = []
    for dr in range(rpb_h):
        t = jnp.zeros((w, w), F32)
        for j in range(rpb_w):
            t = jnp.where(dc_masks[j], rpb_ref[(h * rpb_h + dr) * rpb_w + j], t)
        tiles.append(jnp.where(col_ok, t * inv_scale, NEG_INF))
    blank = jnp.full((w, w), NEG_INF, F32)
    for i in range(grid_rows // NB_QROWS):
        k0 = _nb_slab_start(i, grid_rows)
        for qr in range(NB_QROWS):
            r = i * NB_QROWS + qr
            rs = min(max(r - NA_KH_MAX // 2, 0), grid_rows - NA_KH_MAX)
            row = []
            for kr in range(NB_KROWS):
                key_row = k0 + kr
                ok = rs <= key_row < rs + NA_KH_MAX
                row.append(tiles[key_row - r + NA_KH_MAX - 1] if ok else blank)
            o_ref[i, qr * w:(qr + 1) * w, :] = jnp.concatenate(row, axis=1)


def nb_bias(rpb, grid_rows):
    n_heads = rpb.shape[0]
    nqb = grid_rows // NB_QROWS
    tq, tk = NB_QROWS * GRID_W, NB_KROWS * GRID_W
    return pl.pallas_call(
        functools.partial(_nb_bias_kernel, grid_rows=grid_rows, inv_scale=HEAD_DIM ** 0.5),
        grid=(n_heads,),
        in_specs=[pl.BlockSpec(memory_space=pltpu.SMEM)],
        out_specs=pl.BlockSpec((None, nqb, tq, tk), lambda h: (h, 0, 0, 0)),
        out_shape=jax.ShapeDtypeStruct((n_heads, nqb, tq, tk), F32),
        compiler_params=_params("arbitrary"),
        name="nb_bias",
    )(rpb.reshape(-1))


def _attn_b_kernel(q_ref, kc_ref, kl_ref, vc_ref, vl_ref, bias_ref, o_ref, *, nh, scale, grid_rows):
    i = pl.program_id(1)
    tk = NB_KROWS * GRID_W
    start = pl.multiple_of(_nb_slab_start(i, grid_rows) * GRID_W, GRID_W)
    for h in range(nh):
        hs = slice(h * HEAD_DIM, (h + 1) * HEAD_DIM)
        q = q_ref[:, hs]
        s_c = _dot_nt(q, kc_ref[:, hs])
        s_w = _dot_nt(q, kl_ref[pl.ds(start, tk), hs]) + bias_ref[h]
        o = _softmax_pv([s_c, s_w], [vc_ref[:, hs], vl_ref[pl.ds(start, tk), hs]], scale)
        o_ref[:, hs] = o.astype(o_ref.dtype)


def attn_b(rows, px, q0, k0, v0, bias, nh=2):
    grid_rows = rows.seq // GRID_W
    tq = NB_QROWS * GRID_W
    nqb = rows.seq // tq
    hw = nh * HEAD_DIM
    lat0 = rows.nc // rows.seq
    q_map = lambda g, i, b: (rows.nc // tq + b * nqb + i, q0 // hw + g)
    return pl.pallas_call(
        functools.partial(_attn_b_kernel, nh=nh, scale=HEAD_DIM ** -0.5, grid_rows=grid_rows),
        grid=(B_HEADS // nh, nqb, rows.batch),
        in_specs=[pl.BlockSpec((tq, hw), q_map),
                  pl.BlockSpec((rows.ctx_len, hw), lambda g, i, b: (b, k0 // hw + g)),
                  pl.BlockSpec((rows.seq, hw), lambda g, i, b: (lat0 + b, k0 // hw + g)),
                  pl.BlockSpec((rows.ctx_len, hw), lambda g, i, b: (b, v0 // hw + g)),
                  pl.BlockSpec((rows.seq, hw), lambda g, i, b: (lat0 + b, v0 // hw + g)),
                  pl.BlockSpec((nh, None, tq, NB_KROWS * GRID_W), lambda g, i, b: (g, i, 0, 0))],
        out_specs=pl.BlockSpec((tq, hw), lambda g, i, b: (rows.nc // tq + b * nqb + i, g)),
        out_shape=jax.ShapeDtypeStruct((rows.n, B_HEADS * HEAD_DIM), BF16),
        compiler_params=_params("arbitrary", "arbitrary", "arbitrary"),
        name="attn_b",
    )(px, px, px, px, px, bias)


def _merge_kernel(oa_ref, ob_ref, wa_ref, wb_ref, ga_ref, gb_ref, o_ref):
    a = _dot(oa_ref[...], wa_ref[...])
    b = _dot(ob_ref[...], wb_ref[...])
    y = _sigmoid(ga_ref[...].astype(F32)) * a + _sigmoid(gb_ref[...].astype(F32)) * b
    o_ref[...] = y.astype(o_ref.dtype)


def merge(oa, ob, wa, wb, l, px, ga0, gb0, row0, tm=1024, tn=1024):
    n, k = oa.shape
    d = wa.shape[2]
    t0 = row0 // tm
    return pl.pallas_call(
        _merge_kernel,
        grid=((n - row0) // tm, d // tn),
        in_specs=[pl.BlockSpec((tm, k), lambda i, j: (i + t0, 0)),
                  pl.BlockSpec((tm, k), lambda i, j: (i + t0, 0)),
                  pl.BlockSpec((None, k, tn), lambda i, j: (l, 0, j)),
                  pl.BlockSpec((None, k, tn), lambda i, j: (l, 0, j)),
                  pl.BlockSpec((tm, tn), lambda i, j: (i + t0, ga0 // tn + j)),
                  pl.BlockSpec((tm, tn), lambda i, j: (i + t0, gb0 // tn + j))],
        out_specs=pl.BlockSpec((tm, tn), lambda i, j: (i + t0, j)),
        out_shape=jax.ShapeDtypeStruct((n, d), BF16),
        compiler_params=_params("arbitrary", "arbitrary"),
        name="merge",
    )(oa, ob, wa, wb, px, px)


def _resid_mm_kernel(y_ref, w_ref, x_ref, g_ref, o_ref):
    o_ref[...] = x_ref[...] + g_ref[...] * _dot(y_ref[...], w_ref[...])


def resid_matmul(rows, y, w, l, x, gate, row0, tm, tn=1024):
    n, k = y.shape
    d = w.shape[2]
    t0 = row0 // tm
    return pl.pallas_call(
        _resid_mm_kernel,
        grid=((n - row0) // tm, d // tn),
        in_specs=[pl.BlockSpec((tm, k), lambda i, j: (i + t0, 0)),
                  pl.BlockSpec((None, k, tn), lambda i, j: (l, 0, j)),
                  pl.BlockSpec((tm, tn), lambda i, j: (i + t0, j)),
                  pl.BlockSpec((None, 1, tn), lambda i, j: (rows.mod_index((i + t0) * tm), 0, j))],
        out_specs=pl.BlockSpec((tm, tn), lambda i, j: (i + t0, j)),
        out_shape=jax.ShapeDtypeStruct(x.shape, x.dtype),
        input_output_aliases={2: 0},
        compiler_params=_params("arbitrary", "arbitrary"),
        name="resid_matmul",
    )(y, w, x, gate)


def _pack_words(v):
    half = v.shape[1] // 2
    bits = lax.bitcast_convert_type(v.astype(BF16).astype(F32), U32)
    return (bits[:, :half] >> 16) | bits[:, half:]


def _unpack_words(w):
    lo = lax.bitcast_convert_type(w << 16, F32)
    hi = lax.bitcast_convert_type(w & U32(0xFFFF0000), F32)
    return lo, hi


def _store_token_major(ref, words):
    rows, nch = words.shape[0], words.shape[1] // LANES
    for c in range(nch):
        ref[pl.ds(c, rows, stride=nch), :] = words[:, c * LANES:(c + 1) * LANES]


def _load_token_major(ref, rows, nch):
    return [ref[pl.ds(c, rows, stride=nch), :] for c in range(nch)]


def _route(logits, bias, n_experts):
    lane = lax.broadcasted_iota(jnp.int32, logits.shape, 1)
    lane_f = lane.astype(F32)
    per_group = n_experts // N_GROUPS
    valid = lane < n_experts
    scores = _sigmoid(logits)
    choice = jnp.where(valid, scores + bias, NEG_INF)
    big = float(LANES)

    def first_argmax(x):
        m = jnp.max(x, axis=-1, keepdims=True)
        idx = jnp.min(jnp.where(x == m, lane_f, big), axis=-1, keepdims=True)
        return m, idx

    in_group = [(lane >= g * per_group) & (lane < (g + 1) * per_group) for g in range(N_GROUPS)]
    group_score = []
    for g in range(N_GROUPS):
        xg = jnp.where(in_group[g], choice, NEG_INF)
        m1, i1 = first_argmax(xg)
        m2 = jnp.max(jnp.where(lane_f == i1, NEG_INF, xg), axis=-1, keepdims=True)
        group_score.append(m1 + m2)
    ok = jnp.zeros(logits.shape, F32)
    for g in range(N_GROUPS):
        ahead = jnp.zeros(group_score[g].shape, F32)
        for o in range(N_GROUPS):
            if o != g:
                wins = group_score[o] >= group_score[g] if o < g else group_score[o] > group_score[g]
                ahead = ahead + jnp.where(wins, 1.0, 0.0)
        ok = jnp.where(in_group[g], jnp.where(ahead < TOPK_GROUPS, 1.0, 0.0), ok)
    x = jnp.where(ok > 0.5, choice, NEG_INF)
    idxs, ws = [], []
    for _ in range(TOP_K):
        _, idx = first_argmax(x)
        hit = lane_f == idx
        idxs.append(idx)
        ws.append(jnp.sum(jnp.where(hit, scores, 0.0), axis=-1, keepdims=True))
        x = jnp.where(hit, NEG_INF, x)
    total = functools.reduce(jnp.add, ws)
    idx_tile = jnp.zeros(logits.shape, F32)
    w_tile = jnp.zeros(logits.shape, F32)
    for j in range(TOP_K):
        idx_tile = jnp.where(lane == j, idxs[j], idx_tile)
        w_tile = jnp.where(lane == j, ws[j] / total * ROUTED_SCALE, w_tile)
    return idx_tile.astype(jnp.int32), w_tile


def _norm_router_kernel(x_ref, g_ref, sc_ref, sh_ref, wh_ref, wl_ref, br_ref,
                        h_ref, htok_ref, idx_ref, wgt_ref, *, n_experts):
    y = _rmsnorm_rows(x_ref[...], g_ref[...])
    h = y * (1.0 + sc_ref[...]) + sh_ref[...]
    h_hi = h.astype(BF16)
    h_ref[...] = h_hi
    _store_token_major(htok_ref, _pack_words(h))
    h_lo = (h - h_hi.astype(F32)).astype(BF16)
    wh = wh_ref[...]
    logits = _dot(h_hi, wh) + _dot(h_lo, wh) + _dot(h_hi, wl_ref[...])
    idx_ref[...], wgt_ref[...] = _route(logits, br_ref[...], n_experts)


def norm_router(rows, x, g, sc, sh, w_router, b_router, row0, tm=256):
    n, d = x.shape
    e = w_router.shape[1]
    tok_rows = d // 2 // LANES
    t0 = row0 // tm
    wr = jnp.zeros((d, LANES), F32).at[:, :e].set(w_router)
    wr_hi = wr.astype(BF16)
    wr_lo = (wr - wr_hi.astype(F32)).astype(BF16)
    br = jnp.zeros((1, LANES), F32).at[0, :e].set(b_router)
    mod_map = lambda i: (rows.mod_index((i + t0) * tm), 0, 0)
    const = lambda i: (0, 0)
    row_map = lambda i: (i + t0, 0)
    return pl.pallas_call(
        functools.partial(_norm_router_kernel, n_experts=e),
        grid=((n - row0) // tm,),
        in_specs=[pl.BlockSpec((tm, d), row_map),
                  pl.BlockSpec((1, d), const),
                  pl.BlockSpec((None, 1, d), mod_map),
                  pl.BlockSpec((None, 1, d), mod_map),
                  pl.BlockSpec((d, LANES), const),
                  pl.BlockSpec((d, LANES), const),
                  pl.BlockSpec((1, LANES), const)],
        out_specs=[pl.BlockSpec((tm, d), row_map),
                   pl.BlockSpec((tm * tok_rows, LANES), row_map),
                   pl.BlockSpec((tm, LANES), row_map),
                   pl.BlockSpec((tm, LANES), row_map)],
        out_shape=[jax.ShapeDtypeStruct((n, d), BF16),
                   jax.ShapeDtypeStruct((n * tok_rows, LANES), U32),
                   jax.ShapeDtypeStruct((n, LANES), jnp.int32),
                   jax.ShapeDtypeStruct((n, LANES), F32)],
        compiler_params=_params("arbitrary"),
        name="norm_router",
    )(x, g.reshape(1, d), sc, sh, wr_hi, wr_lo, br)


def moe_plan(idx, wgt, tok0, n_experts, n_rows, tok_rows):
    t, p = idx.shape[0], idx.shape[0] * TOP_K
    n_tiles = p // MOE_TM + n_experts
    assert n_tiles % 2 == 0 and p % MOE_TM == 0 and 2 * MOE_TM <= n_rows
    flat_e = idx.reshape(p)
    order = jnp.argsort(flat_e, stable=True).astype(jnp.int32)
    counts = jnp.sum((flat_e[:, None] == jnp.arange(n_experts, dtype=jnp.int32)[None, :]).astype(jnp.int32), axis=0)
    tiles_per = (counts + MOE_TM - 1) // MOE_TM
    tile_end = jnp.cumsum(tiles_per)
    tile_start = tile_end - tiles_per
    pair_start = jnp.cumsum(counts) - counts
    tile = jnp.arange(n_tiles, dtype=jnp.int32)
    te = jnp.sum((tile[:, None] >= tile_end[None, :]).astype(jnp.int32), axis=1)
    te = jnp.minimum(te, n_experts - 1)
    lane = jnp.arange(MOE_TM, dtype=jnp.int32)[None, :]
    r = (tile - tile_start[te])[:, None] * MOE_TM + lane
    valid = (r < counts[te][:, None]) & (tile < tile_end[-1])[:, None]
    pair = order[jnp.clip(pair_start[te][:, None] + r, 0, p - 1)]
    tok = pair // TOP_K + tok0
    plane = pair % TOP_K
    slot = tile[:, None] * MOE_TM + lane
    src = jnp.where(valid, tok, tok0 + slot % t)
    dump = TOP_K * n_rows + (tile % 2)[:, None] * MOE_TM + lane
    dst = jnp.where(valid, plane * n_rows + tok, dump)
    wrow = jnp.where(valid, wgt.reshape(p)[pair], 0.0).reshape(n_tiles * MOE_TM, 1)
    src, dst = src * tok_rows, dst * tok_rows
    pad = jnp.zeros((n_tiles // 2, IDX_ROWS - 2, MOE_TM), jnp.int32)
    blk = jnp.concatenate([jnp.stack([src[0::2], src[1::2]], axis=1), pad,
                           jnp.stack([dst[0::2], dst[1::2]], axis=1), pad], axis=1).astype(jnp.int32)
    used_steps = ((tile_end[-1] + 1) // 2).astype(jnp.int32).reshape(1)
    return te, used_steps, blk, wrow


IDX_ROWS = 8


def _stage_pitch(tok_rows):
    return tok_rows if (tok_rows // 8) % 2 == 1 else tok_rows + 8


def _moe_expert_kernel(te_ref, ns_ref, idx_hbm, htok_hbm, wrow_ref,
                       wga_ref, wua_ref, wda_ref, wgb_ref, wub_ref, wdb_ref,
                       yt_hbm,
                       src_s, dst_s, in_a, in_b, out_a, out_b, idx_sem, in_sem, out_sem,
                       *, n_rows, tok_rows):
    del te_ref
    n_steps = ns_ref[0]
    k = pl.program_id(0)
    tm, nch = MOE_TM, tok_rows
    pitch = _stage_pitch(nch)
    stage_in, stage_out = (in_a, in_b), (out_a, out_b)
    dump0 = TOP_K * n_rows * nch

    def src_copy(step):
        return pltpu.make_async_copy(idx_hbm.at[step, pl.ds(0, IDX_ROWS), :], src_s, idx_sem.at[0])

    def dst_copy(step):
        return pltpu.make_async_copy(idx_hbm.at[step, pl.ds(IDX_ROWS, IDX_ROWS), :], dst_s, idx_sem.at[1])

    def issue_gathers(tile):
        for r in range(tm):
            off = pl.multiple_of(src_s[tile, r], nch)
            pltpu.make_async_copy(htok_hbm.at[pl.ds(off, nch), :],
                                  stage_in[tile].at[pl.ds(r * pitch, nch), :], in_sem.at[tile]).start()

    def wait_gathers(tile):
        pltpu.make_async_copy(htok_hbm.at[pl.ds(0, tm * nch), :], stage_in[tile].at[pl.ds(0, tm * nch), :],
                              in_sem.at[tile]).wait()

    def issue_scatters(tile):
        for r in range(tm):
            off = pl.multiple_of(dst_s[tile, r], nch)
            pltpu.make_async_copy(stage_out[tile].at[pl.ds(r * pitch, nch), :],
                                  yt_hbm.at[pl.ds(off, nch), :], out_sem.at[tile]).start()

    def bulk_scatter(tile):
        return pltpu.make_async_copy(stage_out[tile].at[pl.ds(0, tm * nch), :],
                                     yt_hbm.at[pl.ds(dump0 + tile * tm * nch, tm * nch), :], out_sem.at[tile])

    def load_rows(tile):
        los, his = [], []
        for c in range(nch):
            lo, hi = _unpack_words(stage_in[tile][pl.ds(c, tm, stride=pitch), :])
            los.append(lo.astype(BF16))
            his.append(hi.astype(BF16))
        return jnp.concatenate(los + his, axis=1)

    def compute(tile, wg_ref, wu_ref, wd_ref):
        x = load_rows(tile)
        hdim = wg_ref.shape[1]
        wgu = jnp.concatenate([wg_ref[...].astype(BF16), wu_ref[...].astype(BF16)], axis=1)
        gu = _dot(x, wgu)
        g, u = gu[:, :hdim], gu[:, hdim:]
        hid = (g * _sigmoid(g)) * u * wrow_ref[tile * tm:(tile + 1) * tm, :]
        y = _dot(hid.astype(BF16), wd_ref[...].astype(BF16))
        words = _pack_words(y)
        for c in range(nch):
            stage_out[tile][pl.ds(c, tm, stride=pitch), :] = words[:, c * LANES:(c + 1) * LANES]

    @pl.when(k < n_steps)
    def _():
        last = k == n_steps - 1
        next_step = jnp.where(last, 0, k + 1)

        @pl.when(k == 0)
        def _():
            first = src_copy(0)
            first.start()
            first.wait()
            issue_gathers(0)
            issue_gathers(1)
            src_copy(next_step).start()
            dst_copy(0).start()
            for tile in range(2):
                stage_out[tile][...] = jnp.zeros(stage_out[tile].shape, U32)
                bulk_scatter(tile).start()

        wait_gathers(0)
        bulk_scatter(0).wait()
        compute(0, wga_ref, wua_ref, wda_ref)
        dst_copy(k).wait()
        issue_scatters(0)
        src_copy(next_step).wait()
        issue_gathers(0)

        wait_gathers(1)
        bulk_scatter(1).wait()
        compute(1, wgb_ref, wub_ref, wdb_ref)
        issue_scatters(1)
        issue_gathers(1)

        @pl.when(jnp.logical_not(last))
        def _():
            src_copy(jnp.where(k + 2 < n_steps, k + 2, 0)).start()
            dst_copy(k + 1).start()

        @pl.when(last)
        def _():
            for tile in range(2):
                wait_gathers(tile)
                bulk_scatter(tile).wait()


def moe_experts(htok, te, used_steps, blk, wrow, wg, wu, wd, l, n_rows):
    n_steps = blk.shape[0]
    tok_rows = htok.shape[0] // n_rows
    _, _, d, hdim = wg.shape
    stage = pltpu.VMEM((MOE_TM * _stage_pitch(tok_rows), LANES), U32)
    w_in_spec = lambda off: pl.BlockSpec((None, None, d, hdim), lambda k, te, ns: (l, te[2 * k + off], 0, 0))
    w_out_spec = lambda off: pl.BlockSpec((None, None, hdim, d), lambda k, te, ns: (l, te[2 * k + off], 0, 0))
    return pl.pallas_call(
        functools.partial(_moe_expert_kernel, n_rows=n_rows, tok_rows=tok_rows),
        grid_spec=pltpu.PrefetchScalarGridSpec(
            num_scalar_prefetch=2,
            grid=(n_steps,),
            in_specs=[pl.BlockSpec(memory_space=pl.ANY),
                      pl.BlockSpec(memory_space=pl.ANY),
                      pl.BlockSpec((2 * MOE_TM, 1), lambda k, te, ns: (k, 0)),
                      w_in_spec(0), w_in_spec(0), w_out_spec(0),
                      w_in_spec(1), w_in_spec(1), w_out_spec(1)],
            out_specs=pl.BlockSpec(memory_space=pl.ANY),
            scratch_shapes=[pltpu.SMEM((IDX_ROWS, MOE_TM), jnp.int32),
                            pltpu.SMEM((IDX_ROWS, MOE_TM), jnp.int32),
                            stage, stage, stage, stage,
                            pltpu.SemaphoreType.DMA((2,)),
                            pltpu.SemaphoreType.DMA((2,)),
                            pltpu.SemaphoreType.DMA((2,))],
        ),
        out_shape=jax.ShapeDtypeStruct(((TOP_K + 1) * n_rows * tok_rows, LANES), U32),
        compiler_params=_params("arbitrary"),
        name="moe_experts",
    )(te, used_steps, blk, htok, wrow, wg, wu, wd, wg, wu, wd)


def _moe_combine_kernel(yt_ref, h_ref, wsg_ref, wsu_ref, wsd_ref, x_ref, g_ref, o_ref, acc_lo, acc_hi, *, tok_rows):
    tm = x_ref.shape[0]
    lo, hi = _unpack_words(yt_ref[0])
    for j in range(1, TOP_K):
        l2, h2 = _unpack_words(yt_ref[j])
        lo, hi = lo + l2, hi + h2
    acc_lo[...] = lo
    acc_hi[...] = hi
    routed = jnp.concatenate(_load_token_major(acc_lo, tm, tok_rows) + _load_token_major(acc_hi, tm, tok_rows), axis=1)
    h = h_ref[...]
    g = _dot(h, wsg_ref[...])
    u = _dot(h, wsu_ref[...])
    shared = _dot(((g * _sigmoid(g)) * u).astype(BF16), wsd_ref[...])
    o_ref[...] = x_ref[...] + g_ref[...] * (routed + shared)


def moe_combine(rows, yt, h, wsg, wsu, wsd, l, x, gate, row0, tm=128):
    n, d = x.shape
    tok_rows = d // 2 // LANES
    yt = yt.reshape(TOP_K + 1, n * tok_rows, LANES)
    s = wsg.shape[2]
    t0 = row0 // tm
    row_map = lambda i: (i + t0, 0)
    return pl.pallas_call(
        functools.partial(_moe_combine_kernel, tok_rows=tok_rows),
        grid=((n - row0) // tm,),
        in_specs=[pl.BlockSpec((TOP_K, tm * tok_rows, LANES), lambda i: (0, i + t0, 0)),
                  pl.BlockSpec((tm, d), row_map),
                  pl.BlockSpec((None, d, s), lambda i: (l, 0, 0)),
                  pl.BlockSpec((None, d, s), lambda i: (l, 0, 0)),
                  pl.BlockSpec((None, s, d), lambda i: (l, 0, 0)),
                  pl.BlockSpec((tm, d), row_map),
                  pl.BlockSpec((None, 1, d), lambda i: (rows.mod_index((i + t0) * tm), 0, 0))],
        out_specs=pl.BlockSpec((tm, d), row_map),
        out_shape=jax.ShapeDtypeStruct(x.shape, x.dtype),
        scratch_shapes=[pltpu.VMEM((tm * tok_rows, LANES), F32), pltpu.VMEM((tm * tok_rows, LANES), F32)],
        input_output_aliases={5: 0},
        compiler_params=_params("arbitrary"),
        name="moe_combine",
    )(yt, h, wsg, wsu, wsd, x, gate)


def kernel(x, c, ctx, c_ctx, g_attn, g_ffn, w_mod_down, w_mod_up, b_mod_up, w_in, q_norm, k_norm, rpb, w_branch_a, w_branch_b, w_out, w_router, b_router, w_exp_gate, w_exp_up, w_exp_down, w_sh_gate, w_sh_up, w_sh_down, g_final):
    batch, seq, d = x.shape
    ctx_len = ctx.shape[1]
    depth = w_in.shape[0]
    n_experts = w_exp_gate.shape[1]
    rows = Rows(batch, ctx_len, seq)
    assert rows.nc % rows.seq == 0 and batch < MOD_ROWS

    a_q, a_kv, b_w = A_HEADS * HEAD_DIM, A_KV_HEADS * HEAD_DIM, B_HEADS * HEAD_DIM
    col_qa = 0
    col_qb = col_qa + a_q
    col_ka = col_qb + b_w
    col_va = col_ka + a_kv
    col_kb = col_va + a_kv
    col_vb = col_kb + b_w
    col_ga = col_vb + b_w
    col_gb = col_ga + d

    cvec = jnp.zeros((MOD_ROWS, d), F32).at[:batch].set(c).at[batch].set(c_ctx)
    mod = modulation_all(cvec, w_mod_down, w_mod_up, b_mod_up)

    w_in_b, w_a_b, w_b_b, w_out_b = (w.astype(BF16) for w in (w_in, w_branch_a, w_branch_b, w_out))
    wsg_b, wsu_b, wsd_b = (w.astype(BF16) for w in (w_sh_gate, w_sh_up, w_sh_down))

    prep_tm = 256
    cos, sin = rope_tables(rows, prep_tm)
    xs = jnp.concatenate([ctx.reshape(rows.nc, d), x.reshape(rows.nx, d)], axis=0)

    for l in range(depth):
        update_ctx = l < depth - 1
        row0 = 0 if update_ctx else rows.nc
        m = [mod[l, :batch + 1, k * d:(k + 1) * d][:, None, :] for k in range(N_MOD)]
        sh1, sc1, gt1, sh2, sc2, gt2 = m

        h1 = norm_mod(rows, xs, g_attn[l], sc1, sh1, 0)
        px = matmul(h1, w_in_b, l)
        qa, ka = qk_prep(rows, px, (col_qa, a_q), (col_ka, a_kv), q_norm[l], k_norm[l], cos, sin, prep_tm)
        oa = attn_a(rows, qa, ka, px, col_va)
        ob = attn_b(rows, px, col_qb, col_kb, col_vb, nb_bias(rpb[l], seq // GRID_W))
        if update_ctx:
            oa = attn_ctx(rows, qa, 0, ka, 0, px, col_va, oa, A_HEADS, A_KV_HEADS)
            ob = attn_ctx(rows, px, col_qb, px, col_kb, px, col_vb, ob, B_HEADS, B_HEADS)
        y = merge(oa, ob, w_a_b, w_b_b, l, px, col_ga, col_gb, row0)
        xs = resid_matmul(rows, y, w_out_b, l, xs, gt1, row0, tm=1024)

        h2, htok, idx_t, wgt_t = norm_router(rows, xs, g_ffn[l], sc2, sh2, w_router[l], b_router[l], row0)
        te, used_steps, blk, wrow = moe_plan(idx_t[row0:, :TOP_K], wgt_t[row0:, :TOP_K], row0, n_experts, rows.n, d // 2 // LANES)
        yt = moe_experts(htok, te, used_steps, blk, wrow, w_exp_gate, w_exp_up, w_exp_down, l, rows.n)
        xs = moe_combine(rows, yt, h2, wsg_b, wsu_b, wsd_b, l, xs, gt2, row0)

    return final_norm(xs, g_final, rows.nc).reshape(batch, seq, d)
```

```python
import functools
import math

import jax
import jax.numpy as jnp
from jax import lax
from jax.experimental import pallas as pl
from jax.experimental.pallas import tpu as pltpu

GRID_W = 64
HEAD_DIM = 128
A_HEADS = 16
A_KV_HEADS = 4
B_HEADS = 16
NA_KH_MAX = 8
NA_KW = 16
ROPE_THETA = 10000.0
N_MOD = 6
TOP_K = 8
N_GROUPS = 8
TOPK_GROUPS = 4
ROUTED_SCALE = 2.5
EPS = 1e-6

F32 = jnp.float32
BF16 = jnp.bfloat16
U32 = jnp.uint32
HIGHEST = lax.Precision.HIGHEST
NEG_INF = float("-inf")
LANES = 128

VMEM_LIMIT_BYTES = 56 * 2**20
MOD_ROWS = 16
MOE_TM = 256


def _params(*sem):
    return pltpu.CompilerParams(dimension_semantics=sem, vmem_limit_bytes=VMEM_LIMIT_BYTES)


def _dot(a, b):
    return jnp.dot(a, b, preferred_element_type=F32)


def _dot_nt(a, b):
    return lax.dot_general(a, b, (((1,), (1,)), ((), ())), preferred_element_type=F32)


def _sigmoid(x):
    return 1.0 / (1.0 + jnp.exp(-x))


def _mod_down_kernel(c_ref, w_ref, o_ref):
    c = c_ref[...]
    o_ref[...] = jnp.dot(c * _sigmoid(c), w_ref[...], preferred_element_type=F32, precision=HIGHEST)


def _mod_up_kernel(t_ref, w_ref, b_ref, o_ref):
    o_ref[...] = jnp.dot(t_ref[...], w_ref[...], preferred_element_type=F32, precision=HIGHEST) + b_ref[...]


def modulation_all(cvec, w_down, w_up, b_up, tn=2048):
    depth, d, r = w_down.shape
    n = w_up.shape[2]
    t = pl.pallas_call(
        _mod_down_kernel,
        grid=(depth,),
        in_specs=[pl.BlockSpec((MOD_ROWS, d), lambda l: (0, 0)),
                  pl.BlockSpec((None, d, r), lambda l: (l, 0, 0))],
        out_specs=pl.BlockSpec((None, MOD_ROWS, r), lambda l: (l, 0, 0)),
        out_shape=jax.ShapeDtypeStruct((depth, MOD_ROWS, r), F32),
        compiler_params=_params("arbitrary"),
        name="mod_down",
    )(cvec, w_down)
    return pl.pallas_call(
        _mod_up_kernel,
        grid=(depth, n // tn),
        in_specs=[pl.BlockSpec((None, MOD_ROWS, r), lambda l, j: (l, 0, 0)),
                  pl.BlockSpec((None, r, tn), lambda l, j: (l, 0, j)),
                  pl.BlockSpec((None, 1, tn), lambda l, j: (l, 0, j))],
        out_specs=pl.BlockSpec((None, MOD_ROWS, tn), lambda l, j: (l, 0, j)),
        out_shape=jax.ShapeDtypeStruct((depth, MOD_ROWS, n), F32),
        compiler_params=_params("arbitrary", "arbitrary"),
        name="mod_up",
    )(t, w_up, b_up.reshape(depth, 1, n))


class Rows:
    def __init__(self, batch, ctx_len, seq):
        self.batch, self.ctx_len, self.seq = batch, ctx_len, seq
        self.nc = batch * ctx_len
        self.nx = batch * seq
        self.n = self.nc + self.nx

    def mod_index(self, row):
        return jnp.where(row < self.nc, self.batch, (row - self.nc) // self.seq)


def _rmsnorm_rows(x, g):
    ms = jnp.mean(x * x, axis=-1, keepdims=True)
    return x * lax.rsqrt(ms + EPS) * g


def _norm_mod_kernel(x_ref, g_ref, sc_ref, sh_ref, o_ref):
    y = _rmsnorm_rows(x_ref[...], g_ref[...])
    o_ref[...] = (y * (1.0 + sc_ref[...]) + sh_ref[...]).astype(o_ref.dtype)


def norm_mod(rows, x, g, sc, sh, row0, tm=256):
    n, d = x.shape
    t0 = row0 // tm
    mod_map = lambda i: (rows.mod_index((i + t0) * tm), 0, 0)
    return pl.pallas_call(
        _norm_mod_kernel,
        grid=((n - row0) // tm,),
        in_specs=[pl.BlockSpec((tm, d), lambda i: (i + t0, 0)),
                  pl.BlockSpec((1, d), lambda i: (0, 0)),
                  pl.BlockSpec((None, 1, d), mod_map),
                  pl.BlockSpec((None, 1, d), mod_map)],
        out_specs=pl.BlockSpec((tm, d), lambda i: (i + t0, 0)),
        out_shape=jax.ShapeDtypeStruct((n, d), BF16),
        compiler_params=_params("arbitrary"),
        name="norm_mod",
    )(x, g.reshape(1, d), sc, sh)


def _mm_kernel(x_ref, w_ref, o_ref):
    o_ref[...] = _dot(x_ref[...], w_ref[...]).astype(o_ref.dtype)


def matmul(x, w, l, tm=1024, tn=1024):
    m, k = x.shape
    n = w.shape[2]
    return pl.pallas_call(
        _mm_kernel,
        grid=(m // tm, n // tn),
        in_specs=[pl.BlockSpec((tm, k), lambda i, j: (i, 0)),
                  pl.BlockSpec((None, k, tn), lambda i, j: (l, 0, j))],
        out_specs=pl.BlockSpec((tm, tn), lambda i, j: (i, j)),
        out_shape=jax.ShapeDtypeStruct((m, n), BF16),
        compiler_params=_params("arbitrary", "arbitrary"),
        name="matmul",
    )(x, w)


def rope_tables(rows, tm):
    half = HEAD_DIM // 2
    quarter = half // 2
    pos = jnp.arange(rows.seq)
    inv_freq = ROPE_THETA ** (-jnp.arange(quarter, dtype=F32) / quarter)
    ang_r = (pos // GRID_W).astype(F32)[:, None] * inv_freq[None, :]
    ang_c = (pos % GRID_W).astype(F32)[:, None] * inv_freq[None, :]
    ang = jnp.concatenate([ang_r, ang_r, ang_c, ang_c], axis=-1)
    sign = jnp.concatenate([-jnp.ones(quarter), jnp.ones(quarter)] * 2).astype(F32)
    cos = jnp.concatenate([jnp.ones((tm, HEAD_DIM), F32), jnp.cos(ang)], axis=0)
    sin = jnp.concatenate([jnp.zeros((tm, HEAD_DIM), F32), jnp.sin(ang) * sign[None, :]], axis=0)
    return cos, sin


def _qk_prep_kernel(q_ref, k_ref, cos_ref, sin_ref, qg_ref, kg_ref, qo_ref, ko_ref):
    cos = cos_ref[...]
    sin = sin_ref[...]
    quarter = HEAD_DIM // 4
    lane = lax.broadcasted_iota(jnp.int32, cos.shape, 1)
    first = (lane & quarter) == 0

    def prep(x_ref, g, o_ref):
        for h in range(x_ref.shape[1] // HEAD_DIM):
            sl = slice(h * HEAD_DIM, (h + 1) * HEAD_DIM)
            y = _rmsnorm_rows(x_ref[:, sl].astype(F32), g)
            partner = jnp.where(first, pltpu.roll(y, HEAD_DIM - quarter, 1), pltpu.roll(y, quarter, 1))
            o_ref[:, sl] = (y * cos + partner * sin).astype(o_ref.dtype)

    prep(q_ref, qg_ref[...], qo_ref)
    prep(k_ref, kg_ref[...], ko_ref)


def qk_prep(rows, px, q_cols, k_cols, q_norm, k_norm, cos, sin, tm=256):
    n = px.shape[0]
    (q0, qw), (k0, kw) = q_cols, k_cols
    per_seq = rows.seq // tm

    def tab_map(i):
        r = i * tm
        return (jnp.where(r < rows.nc, 0, 1 + ((r - rows.nc) // tm) % per_seq), 0)

    return pl.pallas_call(
        _qk_prep_kernel,
        grid=(n // tm,),
        in_specs=[pl.BlockSpec((tm, qw), lambda i: (i, q0 // qw)),
                  pl.BlockSpec((tm, kw), lambda i: (i, k0 // kw)),
                  pl.BlockSpec((tm, HEAD_DIM), tab_map),
                  pl.BlockSpec((tm, HEAD_DIM), tab_map),
                  pl.BlockSpec((1, HEAD_DIM), lambda i: (0, 0)),
                  pl.BlockSpec((1, HEAD_DIM), lambda i: (0, 0))],
        out_specs=[pl.BlockSpec((tm, qw), lambda i: (i, 0)),
                   pl.BlockSpec((tm, kw), lambda i: (i, 0))],
        out_shape=[jax.ShapeDtypeStruct((n, qw), BF16), jax.ShapeDtypeStruct((n, kw), BF16)],
        compiler_params=_params("arbitrary"),
        name="qk_prep",
    )(px, px, cos, sin, q_norm.reshape(1, HEAD_DIM), k_norm.reshape(1, HEAD_DIM))


def _softmax_pv(scores, values, scale):
    c = scale * math.log2(math.e)
    m = functools.reduce(jnp.maximum, [jnp.max(s, axis=-1, keepdims=True) for s in scores])
    ps = [jnp.exp2((s - m) * c) for s in scores]
    l = functools.reduce(jnp.add, [jnp.sum(p, axis=-1, keepdims=True) for p in ps])
    o = functools.reduce(jnp.add, [_dot(p.astype(v.dtype), v) for p, v in zip(ps, values)])
    return o / l


def _attn_ctx_kernel(q_ref, k_ref, v_ref, prev_ref, o_ref, *, nq, nk, scale):
    del prev_ref
    for h in range(nq):
        kh = h * nk // nq
        ks = slice(kh * HEAD_DIM, (kh + 1) * HEAD_DIM)
        qs = slice(h * HEAD_DIM, (h + 1) * HEAD_DIM)
        s = _dot_nt(q_ref[:, qs], k_ref[:, ks])
        o_ref[:, qs] = _softmax_pv([s], [v_ref[:, ks]], scale).astype(o_ref.dtype)


def attn_ctx(rows, q_arr, q0, k_arr, k0, v_arr, v0, out_prev, n_heads, n_kv_heads, nq=4):
    nk = nq * n_kv_heads // n_heads
    qw, kw = nq * HEAD_DIM, nk * HEAD_DIM
    cl = rows.ctx_len
    return pl.pallas_call(
        functools.partial(_attn_ctx_kernel, nq=nq, nk=nk, scale=HEAD_DIM ** -0.5),
        grid=(rows.batch, n_heads // nq),
        in_specs=[pl.BlockSpec((cl, qw), lambda b, g: (b, q0 // qw + g)),
                  pl.BlockSpec((cl, kw), lambda b, g: (b, k0 // kw + g)),
                  pl.BlockSpec((cl, kw), lambda b, g: (b, v0 // kw + g)),
                  pl.BlockSpec(memory_space=pl.ANY)],
        out_specs=pl.BlockSpec((cl, qw), lambda b, g: (b, g)),
        out_shape=jax.ShapeDtypeStruct(out_prev.shape, out_prev.dtype),
        input_output_aliases={3: 0},
        compiler_params=_params("arbitrary", "arbitrary"),
        name="attn_ctx",
    )(q_arr, k_arr, v_arr, out_prev)


def _attn_a_kernel(q_ref, kc_ref, kl_ref, vc_ref, vl_ref, o_ref, *, nq, scale):
    kc, kl, vc, vl = kc_ref[...], kl_ref[...], vc_ref[...], vl_ref[...]
    for h in range(nq):
        qs = slice(h * HEAD_DIM, (h + 1) * HEAD_DIM)
        q = q_ref[:, qs]
        o = _softmax_pv([_dot_nt(q, kc), _dot_nt(q, kl)], [vc, vl], scale)
        o_ref[:, qs] = o.astype(o_ref.dtype)


def attn_a(rows, qa, ka, px, v0, tq=512):
    nq = A_HEADS // A_KV_HEADS
    qw = nq * HEAD_DIM
    n_q_blocks = rows.seq // tq
    lat0 = rows.nc // rows.seq
    return pl.pallas_call(
        functools.partial(_attn_a_kernel, nq=nq, scale=HEAD_DIM ** -0.5),
        grid=(rows.batch, A_KV_HEADS, n_q_blocks),
        in_specs=[pl.BlockSpec((tq, qw), lambda b, g, i: (rows.nc // tq + b * n_q_blocks + i, g)),
                  pl.BlockSpec((rows.ctx_len, HEAD_DIM), lambda b, g, i: (b, g)),
                  pl.BlockSpec((rows.seq, HEAD_DIM), lambda b, g, i: (lat0 + b, g)),
                  pl.BlockSpec((rows.ctx_len, HEAD_DIM), lambda b, g, i: (b, v0 // HEAD_DIM + g)),
                  pl.BlockSpec((rows.seq, HEAD_DIM), lambda b, g, i: (lat0 + b, v0 // HEAD_DIM + g))],
        out_specs=pl.BlockSpec((tq, qw), lambda b, g, i: (rows.nc // tq + b * n_q_blocks + i, g)),
        out_shape=jax.ShapeDtypeStruct((rows.n, A_HEADS * HEAD_DIM), BF16),
        compiler_params=_params("arbitrary", "arbitrary", "arbitrary"),
        name="attn_a",
    )(qa, ka, ka, px, px)


NB_QROWS = 8
NB_KROWS = 16


def _nb_slab_start(i, grid_rows):
    lo = i * NB_QROWS - NA_KH_MAX // 2
    hi = grid_rows - NB_KROWS
    if isinstance(i, int):
        return min(max(lo, 0), hi)
    return jnp.clip(lo, 0, hi)


def _nb_bias_kernel(rpb_ref, o_ref, *, grid_rows, inv_scale):
    h = pl.program_id(0)
    w = GRID_W
    rpb_h, rpb_w = 2 * NA_KH_MAX - 1, 2 * NA_KW - 1
    qc = lax.broadcasted_iota(jnp.int32, (w, w), 0)
    kc = lax.broadcasted_iota(jnp.int32, (w, w), 1)
    cs = jnp.clip(qc - NA_KW // 2, 0, w - NA_KW)
    col_ok = (kc >= cs) & (kc < cs + NA_KW)
    dc = kc - qc + NA_KW - 1
    dc_masks = [dc == j for j in range(rpb_w)]
    tiles = []
    for dr in range(rpb_h):
        t = jnp.zeros((w, w), F32)
        for j in range(rpb_w):
            t = jnp.where(dc_masks[j], rpb_ref[(h * rpb_h + dr) * rpb_w + j], t)
        tiles.append(jnp.where(col_ok, t * inv_scale, NEG_INF))
    blank = jnp.full((w, w), NEG_INF, F32)
    for i in range(grid_rows // NB_QROWS):
        k0 = _nb_slab_start(i, grid_rows)
        for qr in range(NB_QROWS):
            r = i * NB_QROWS + qr
            rs = min(max(r - NA_KH_MAX // 2, 0), grid_rows - NA_KH_MAX)
            row = []
            for kr in range(NB_KROWS):
                key_row = k0 + kr
                ok = rs <= key_row < rs + NA_KH_MAX
                row.append(tiles[key_row - r + NA_KH_MAX - 1] if ok else blank)
            o_ref[i, qr * w:(qr + 1) * w, :] = jnp.concatenate(row, axis=1)


def nb_bias(rpb, grid_rows):
    n_heads = rpb.shape[0]
    nqb = grid_rows // NB_QROWS
    tq, tk = NB_QROWS * GRID_W, NB_KROWS * GRID_W
    return pl.pallas_call(
        functools.partial(_nb_bias_kernel, grid_rows=grid_rows, inv_scale=HEAD_DIM ** 0.5),
        grid=(n_heads,),
        in_specs=[pl.BlockSpec(memory_space=pltpu.SMEM)],
        out_specs=pl.BlockSpec((None, nqb, tq, tk), lambda h: (h, 0, 0, 0)),
        out_shape=jax.ShapeDtypeStruct((n_heads, nqb, tq, tk), F32),
        compiler_params=_params("arbitrary"),
        name="nb_bias",
    )(rpb.reshape(-1))


def _attn_b_kernel(q_ref, kc_ref, kl_ref, vc_ref, vl_ref, bias_ref, o_ref, *, nh, scale, grid_rows):
    i = pl.program_id(1)
    tk = NB_KROWS * GRID_W
    start = pl.multiple_of(_nb_slab_start(i, grid_rows) * GRID_W, GRID_W)
    for h in range(nh):
        hs = slice(h * HEAD_DIM, (h + 1) * HEAD_DIM)
        q = q_ref[:, hs]
        s_c = _dot_nt(q, kc_ref[:, hs])
        s_w = _dot_nt(q, kl_ref[pl.ds(start, tk), hs]) + bias_ref[h]
        o = _softmax_pv([s_c, s_w], [vc_ref[:, hs], vl_ref[pl.ds(start, tk), hs]], scale)
        o_ref[:, hs] = o.astype(o_ref.dtype)


def attn_b(rows, px, q0, k0, v0, bias, nh=2):
    grid_rows = rows.seq // GRID_W
    tq = NB_QROWS * GRID_W
    nqb = rows.seq // tq
    hw = nh * HEAD_DIM
    lat0 = rows.nc // rows.seq
    q_map = lambda g, i, b: (rows.nc // tq + b * nqb + i, q0 // hw + g)
    return pl.pallas_call(
        functools.partial(_attn_b_kernel, nh=nh, scale=HEAD_DIM ** -0.5, grid_rows=grid_rows),
        grid=(B_HEADS // nh, nqb, rows.batch),
        in_specs=[pl.BlockSpec((tq, hw), q_map),
                  pl.BlockSpec((rows.ctx_len, hw), lambda g, i, b: (b, k0 // hw + g)),
                  pl.BlockSpec((rows.seq, hw), lambda g, i, b: (lat0 + b, k0 // hw + g)),
                  pl.BlockSpec((rows.ctx_len, hw), lambda g, i, b: (b, v0 // hw + g)),
                  pl.BlockSpec((rows.seq, hw), lambda g, i, b: (lat0 + b, v0 // hw + g)),
                  pl.BlockSpec((nh, None, tq, NB_KROWS * GRID_W), lambda g, i, b: (g, i, 0, 0))],
        out_specs=pl.BlockSpec((tq, hw), lambda g, i, b: (rows.nc // tq + b * nqb + i, g)),
        out_shape=jax.ShapeDtypeStruct((rows.n, B_HEADS * HEAD_DIM), BF16),
        compiler_params=_params("arbitrary", "arbitrary", "arbitrary"),
        name="attn_b",
    )(px, px, px, px, px, bias)


def _merge_kernel(oa_ref, ob_ref, wa_ref, wb_ref, ga_ref, gb_ref, o_ref):
    a = _dot(oa_ref[...], wa_ref[...])
    b = _dot(ob_ref[...], wb_ref[...])
    y = _sigmoid(ga_ref[...].astype(F32)) * a + _sigmoid(gb_ref[...].astype(F32)) * b
    o_ref[...] = y.astype(o_ref.dtype)


def merge(oa, ob, wa, wb, l, px, ga0, gb0, row0, tm=1024, tn=1024):
    n, k = oa.shape
    d = wa.shape[2]
    t0 = row0 // tm
    return pl.pallas_call(
        _merge_kernel,
        grid=((n - row0) // tm, d // tn),
        in_specs=[pl.BlockSpec((tm, k), lambda i, j: (i + t0, 0)),
                  pl.BlockSpec((tm, k), lambda i, j: (i + t0, 0)),
                  pl.BlockSpec((None, k, tn), lambda i, j: (l, 0, j)),
                  pl.BlockSpec((None, k, tn), lambda i, j: (l, 0, j)),
                  pl.BlockSpec((tm, tn), lambda i, j: (i + t0, ga0 // tn + j)),
                  pl.BlockSpec((tm, tn), lambda i, j: (i + t0, gb0 // tn + j))],
        out_specs=pl.BlockSpec((tm, tn), lambda i, j: (i + t0, j)),
        out_shape=jax.ShapeDtypeStruct((n, d), BF16),
        compiler_params=_params("arbitrary", "arbitrary"),
        name="merge",
    )(oa, ob, wa, wb, px, px)


def _resid_mm_kernel(y_ref, w_ref, x_ref, g_ref, o_ref):
    o_ref[...] = x_ref[...] + g_ref[...] * _dot(y_ref[...], w_ref[...])


def resid_matmul(rows, y, w, l, x, gate, row0, tm, tn=1024):
    n, k = y.shape
    d = w.shape[2]
    t0 = row0 // tm
    return pl.pallas_call(
        _resid_mm_kernel,
        grid=((n - row0) // tm, d // tn),
        in_specs=[pl.BlockSpec((tm, k), lambda i, j: (i + t0, 0)),
                  pl.BlockSpec((None, k, tn), lambda i, j: (l, 0, j)),
                  pl.BlockSpec((tm, tn), lambda i, j: (i + t0, j)),
                  pl.BlockSpec((None, 1, tn), lambda i, j: (rows.mod_index((i + t0) * tm), 0, j))],
        out_specs=pl.BlockSpec((tm, tn), lambda i, j: (i + t0, j)),
        out_shape=jax.ShapeDtypeStruct(x.shape, x.dtype),
        input_output_aliases={2: 0},
        compiler_params=_params("arbitrary", "arbitrary"),
        name="resid_matmul",
    )(y, w, x, gate)


def _pack_words(v):
    half = v.shape[1] // 2
    bits = lax.bitcast_convert_type(v.astype(BF16).astype(F32), U32)
    return (bits[:, :half] >> 16) | bits[:, half:]


def _unpack_words(w):
    lo = lax.bitcast_convert_type(w << 16, F32)
    hi = lax.bitcast_convert_type(w & U32(0xFFFF0000), F32)
    return lo, hi


def _store_token_major(ref, words):
    rows, nch = words.shape[0], words.shape[1] // LANES
    for c in range(nch):
        ref[pl.ds(c, rows, stride=nch), :] = words[:, c * LANES:(c + 1) * LANES]


def _load_token_major(ref, rows, nch):
    return [ref[pl.ds(c, rows, stride=nch), :] for c in range(nch)]


def _route(logits, bias, n_experts):
    lane = lax.broadcasted_iota(jnp.int32, logits.shape, 1)
    lane_f = lane.astype(F32)
    per_group = n_experts // N_GROUPS
    valid = lane < n_experts
    scores = _sigmoid(logits)
    choice = jnp.where(valid, scores + bias, NEG_INF)
    big = float(LANES)

    def first_argmax(x):
        m = jnp.max(x, axis=-1, keepdims=True)
        idx = jnp.min(jnp.where(x == m, lane_f, big), axis=-1, keepdims=True)
        return m, idx

    in_group = [(lane >= g * per_group) & (lane < (g + 1) * per_group) for g in range(N_GROUPS)]
    group_score = []
    for g in range(N_GROUPS):
        xg = jnp.where(in_group[g], choice, NEG_INF)
        m1, i1 = first_argmax(xg)
        m2 = jnp.max(jnp.where(lane_f == i1, NEG_INF, xg), axis=-1, keepdims=True)
        group_score.append(m1 + m2)
    ok = jnp.zeros(logits.shape, F32)
    for g in range(N_GROUPS):
        ahead = jnp.zeros(group_score[g].shape, F32)
        for o in range(N_GROUPS):
            if o != g:
                wins = group_score[o] >= group_score[g] if o < g else group_score[o] > group_score[g]
                ahead = ahead + jnp.where(wins, 1.0, 0.0)
        ok = jnp.where(in_group[g], jnp.where(ahead < TOPK_GROUPS, 1.0, 0.0), ok)
    x = jnp.where(ok > 0.5, choice, NEG_INF)
    idxs, ws = [], []
    for _ in range(TOP_K):
        _, idx = first_argmax(x)
        hit = lane_f == idx
        idxs.append(idx)
        ws.append(jnp.sum(jnp.where(hit, scores, 0.0), axis=-1, keepdims=True))
        x = jnp.where(hit, NEG_INF, x)
    total = functools.reduce(jnp.add, ws)
    idx_tile = jnp.zeros(logits.shape, F32)
    w_tile = jnp.zeros(logits.shape, F32)
    for j in range(TOP_K):
        idx_tile = jnp.where(lane == j, idxs[j], idx_tile)
        w_tile = jnp.where(lane == j, ws[j] / total * ROUTED_SCALE, w_tile)
    return idx_tile.astype(jnp.int32), w_tile


def _norm_router_kernel(x_ref, g_ref, sc_ref, sh_ref, wh_ref, wl_ref, br_ref,
                        h_ref, htok_ref, idx_ref, wgt_ref, *, n_experts):
    y = _rmsnorm_rows(x_ref[...], g_ref[...])
    h = y * (1.0 + sc_ref[...]) + sh_ref[...]
    h_hi = h.astype(BF16)
    h_ref[...] = h_hi
    _store_token_major(htok_ref, _pack_words(h))
    h_lo = (h - h_hi.astype(F32)).astype(BF16)
    wh = wh_ref[...]
    logits = _dot(h_hi, wh) + _dot(h_lo, wh) + _dot(h_hi, wl_ref[...])
    idx_ref[...], wgt_ref[...] = _route(logits, br_ref[...], n_experts)


def norm_router(rows, x, g, sc, sh, w_router, b_router, row0, tm=256):
    n, d = x.shape
    e = w_router.shape[1]
    tok_rows = d // 2 // LANES
    t0 = row0 // tm
    wr = jnp.zeros((d, LANES), F32).at[:, :e].set(w_router)
    wr_hi = wr.astype(BF16)
    wr_lo = (wr - wr_hi.astype(F32)).astype(BF16)
    br = jnp.zeros((1, LANES), F32).at[0, :e].set(b_router)
    mod_map = lambda i: (rows.mod_index((i + t0) * tm), 0, 0)
    const = lambda i: (0, 0)
    row_map = lambda i: (i + t0, 0)
    return pl.pallas_call(
        functools.partial(_norm_router_kernel, n_experts=e),
        grid=((n - row0) // tm,),
        in_specs=[pl.BlockSpec((tm, d), row_map),
                  pl.BlockSpec((1, d), const),
                  pl.BlockSpec((None, 1, d), mod_map),
                  pl.BlockSpec((None, 1, d), mod_map),
                  pl.BlockSpec((d, LANES), const),
                  pl.BlockSpec((d, LANES), const),
                  pl.BlockSpec((1, LANES), const)],
        out_specs=[pl.BlockSpec((tm, d), row_map),
                   pl.BlockSpec((tm * tok_rows, LANES), row_map),
                   pl.BlockSpec((tm, LANES), row_map),
                   pl.BlockSpec((tm, LANES), row_map)],
        out_shape=[jax.ShapeDtypeStruct((n, d), BF16),
                   jax.ShapeDtypeStruct((n * tok_rows, LANES), U32),
                   jax.ShapeDtypeStruct((n, LANES), jnp.int32),
                   jax.ShapeDtypeStruct((n, LANES), F32)],
        compiler_params=_params("arbitrary"),
        name="norm_router",
    )(x, g.reshape(1, d), sc, sh, wr_hi, wr_lo, br)


def moe_plan(idx, wgt, tok0, n_experts, n_rows, tok_rows):
    t, p = idx.shape[0], idx.shape[0] * TOP_K
    n_tiles = p // MOE_TM + n_experts
    assert n_tiles % 2 == 0 and p % MOE_TM == 0 and 2 * MOE_TM <= n_rows
    flat_e = idx.reshape(p)
    order = jnp.argsort(flat_e, stable=True).astype(jnp.int32)
    counts = jnp.sum((flat_e[:, None] == jnp.arange(n_experts, dtype=jnp.int32)[None, :]).astype(jnp.int32), axis=0)
    tiles_per = (counts + MOE_TM - 1) // MOE_TM
    tile_end = jnp.cumsum(tiles_per)
    tile_start = tile_end - tiles_per
    pair_start = jnp.cumsum(counts) - counts
    tile = jnp.arange(n_tiles, dtype=jnp.int32)
    te = jnp.sum((tile[:, None] >= tile_end[None, :]).astype(jnp.int32), axis=1)
    te = jnp.minimum(te, n_experts - 1)
    lane = jnp.arange(MOE_TM, dtype=jnp.int32)[None, :]
    r = (tile - tile_start[te])[:, None] * MOE_TM + lane
    valid = (r < counts[te][:, None]) & (tile < tile_end[-1])[:, None]
    pair = order[jnp.clip(pair_start[te][:, None] + r, 0, p - 1)]
    tok = pair // TOP_K + tok0
    plane = pair % TOP_K
    slot = tile[:, None] * MOE_TM + lane
    src = jnp.where(valid, tok, tok0 + slot % t)
    dump = TOP_K * n_rows + (tile % 2)[:, None] * MOE_TM + lane
    dst = jnp.where(valid, plane * n_rows + tok, dump)
    wrow = jnp.where(valid, wgt.reshape(p)[pair], 0.0).reshape(n_tiles * MOE_TM, 1)
    src, dst = src * tok_rows, dst * tok_rows
    pad = jnp.zeros((n_tiles // 2, IDX_ROWS - 2, MOE_TM), jnp.int32)
    blk = jnp.concatenate([jnp.stack([src[0::2], src[1::2]], axis=1), pad,
                           jnp.stack([dst[0::2], dst[1::2]], axis=1), pad], axis=1).astype(jnp.int32)
    used_steps = ((tile_end[-1] + 1) // 2).astype(jnp.int32).reshape(1)
    return te, used_steps, blk, wrow


IDX_ROWS = 8


def _stage_pitch(tok_rows):
    return tok_rows if (tok_rows // 8) % 2 == 1 else tok_rows + 8


def _moe_expert_kernel(te_ref, ns_ref, idx_hbm, htok_hbm, wrow_ref,
                       wga_ref, wua_ref, wda_ref, wgb_ref, wub_ref, wdb_ref,
                       yt_hbm,
                       src_s, dst_s, in_a, in_b, out_a, out_b, wgu_a, wgu_b, wd_a, wd_b,
                       idx_sem, in_sem, out_sem,
                       *, n_rows, tok_rows):
    wgu_cache, wd_cache = (wgu_a, wgu_b), (wd_a, wd_b)
    n_steps = ns_ref[0]
    k = pl.program_id(0)
    tm, nch = MOE_TM, tok_rows
    pitch = _stage_pitch(nch)
    stage_in, stage_out = (in_a, in_b), (out_a, out_b)
    dump0 = TOP_K * n_rows * nch

    def src_copy(step):
        return pltpu.make_async_copy(idx_hbm.at[step, pl.ds(0, IDX_ROWS), :], src_s, idx_sem.at[0])

    def dst_copy(step):
        return pltpu.make_async_copy(idx_hbm.at[step, pl.ds(IDX_ROWS, IDX_ROWS), :], dst_s, idx_sem.at[1])

    def issue_gathers(tile):
        for r in range(tm):
            off = pl.multiple_of(src_s[tile, r], nch)
            pltpu.make_async_copy(htok_hbm.at[pl.ds(off, nch), :],
                                  stage_in[tile].at[pl.ds(r * pitch, nch), :], in_sem.at[tile]).start()

    def wait_gathers(tile):
        pltpu.make_async_copy(htok_hbm.at[pl.ds(0, tm * nch), :], stage_in[tile].at[pl.ds(0, tm * nch), :],
                              in_sem.at[tile]).wait()

    def issue_scatters(tile):
        for r in range(tm):
            off = pl.multiple_of(dst_s[tile, r], nch)
            pltpu.make_async_copy(stage_out[tile].at[pl.ds(r * pitch, nch), :],
                                  yt_hbm.at[pl.ds(off, nch), :], out_sem.at[tile]).start()

    def bulk_scatter(tile):
        return pltpu.make_async_copy(stage_out[tile].at[pl.ds(0, tm * nch), :],
                                     yt_hbm.at[pl.ds(dump0 + tile * tm * nch, tm * nch), :], out_sem.at[tile])

    def load_rows(tile):
        los, his = [], []
        for c in range(nch):
            lo, hi = _unpack_words(stage_in[tile][pl.ds(c, tm, stride=pitch), :])
            los.append(lo.astype(BF16))
            his.append(hi.astype(BF16))
        return jnp.concatenate(los + his, axis=1)

    def compute(tile, wg_ref, wu_ref, wd_ref):
        t = 2 * k + tile
        wgu_c, wd_c = wgu_cache[tile], wd_cache[tile]

        @pl.when((k == 0) | (te_ref[t] != te_ref[jnp.maximum(t - 2, 0)]))
        def _():
            wgu_c[...] = jnp.concatenate([wg_ref[...].astype(BF16), wu_ref[...].astype(BF16)], axis=1)
            wd_c[...] = wd_ref[...].astype(BF16)

        x = load_rows(tile)
        hdim = wg_ref.shape[1]
        gu = _dot(x, wgu_c[...])
        g, u = gu[:, :hdim], gu[:, hdim:]
        hid = (g * _sigmoid(g)) * u * wrow_ref[tile * tm:(tile + 1) * tm, :]
        y = _dot(hid.astype(BF16), wd_c[...])
        words = _pack_words(y)
        for c in range(nch):
            stage_out[tile][pl.ds(c, tm, stride=pitch), :] = words[:, c * LANES:(c + 1) * LANES]

    @pl.when(k < n_steps)
    def _():
        last = k == n_steps - 1
        next_step = jnp.where(last, 0, k + 1)

        @pl.when(k == 0)
        def _():
            first = src_copy(0)
            first.start()
            first.wait()
            issue_gathers(0)
            issue_gathers(1)
            src_copy(next_step).start()
            dst_copy(0).start()
            for tile in range(2):
                stage_out[tile][...] = jnp.zeros(stage_out[tile].shape, U32)
                bulk_scatter(tile).start()

        wait_gathers(0)
        bulk_scatter(0).wait()
        compute(0, wga_ref, wua_ref, wda_ref)
        dst_copy(k).wait()
        issue_scatters(0)
        src_copy(next_step).wait()
        issue_gathers(0)

        wait_gathers(1)
        bulk_scatter(1).wait()
        compute(1, wgb_ref, wub_ref, wdb_ref)
        issue_scatters(1)
        issue_gathers(1)

        @pl.when(jnp.logical_not(last))
        def _():
            src_copy(jnp.where(k + 2 < n_steps, k + 2, 0)).start()
            dst_copy(k + 1).start()

        @pl.when(last)
        def _():
            for tile in range(2):
                wait_gathers(tile)
                bulk_scatter(tile).wait()


def moe_experts(htok, te, used_steps, blk, wrow, wg, wu, wd, l, n_rows):
    n_steps = blk.shape[0]
    tok_rows = htok.shape[0] // n_rows
    _, _, d, hdim = wg.shape
    stage = pltpu.VMEM((MOE_TM * _stage_pitch(tok_rows), LANES), U32)
    w_in_spec = lambda off: pl.BlockSpec((None, None, d, hdim), lambda k, te, ns: (l, te[2 * k + off], 0, 0))
    w_out_spec = lambda off: pl.BlockSpec((None, None, hdim, d), lambda k, te, ns: (l, te[2 * k + off], 0, 0))
    return pl.pallas_call(
        functools.partial(_moe_expert_kernel, n_rows=n_rows, tok_rows=tok_rows),
        grid_spec=pltpu.PrefetchScalarGridSpec(
            num_scalar_prefetch=2,
            grid=(n_steps,),
            in_specs=[pl.BlockSpec(memory_space=pl.ANY),
                      pl.BlockSpec(memory_space=pl.ANY),
                      pl.BlockSpec((2 * MOE_TM, 1), lambda k, te, ns: (k, 0)),
                      w_in_spec(0), w_in_spec(0), w_out_spec(0),
                      w_in_spec(1), w_in_spec(1), w_out_spec(1)],
            out_specs=pl.BlockSpec(memory_space=pl.ANY),
            scratch_shapes=[pltpu.SMEM((IDX_ROWS, MOE_TM), jnp.int32),
                            pltpu.SMEM((IDX_ROWS, MOE_TM), jnp.int32),
                            stage, stage, stage, stage,
                            pltpu.VMEM((d, 2 * hdim), BF16), pltpu.VMEM((d, 2 * hdim), BF16),
                            pltpu.VMEM((hdim, d), BF16), pltpu.VMEM((hdim, d), BF16),
                            pltpu.SemaphoreType.DMA((2,)),
                            pltpu.SemaphoreType.DMA((2,)),
                            pltpu.SemaphoreType.DMA((2,))],
        ),
        out_shape=jax.ShapeDtypeStruct(((TOP_K + 1) * n_rows * tok_rows, LANES), U32),
        compiler_params=_params("arbitrary"),
        name="moe_experts",
    )(te, used_steps, blk, htok, wrow, wg, wu, wd, wg, wu, wd)


def _moe_combine_kernel(yt_ref, h_ref, wsg_ref, wsu_ref, wsd_ref, x_ref, g_ref, ng_ref, nsc_ref, nsh_ref,
                        o_ref, n_ref, acc_lo, acc_hi, *, tok_rows):
    tm = x_ref.shape[0]
    lo, hi = _unpack_words(yt_ref[0])
    for j in range(1, TOP_K):
        l2, h2 = _unpack_words(yt_ref[j])
        lo, hi = lo + l2, hi + h2
    acc_lo[...] = lo
    acc_hi[...] = hi
    routed = jnp.concatenate(_load_token_major(acc_lo, tm, tok_rows) + _load_token_major(acc_hi, tm, tok_rows), axis=1)
    h = h_ref[...]
    g = _dot(h, wsg_ref[...])
    u = _dot(h, wsu_ref[...])
    shared = _dot(((g * _sigmoid(g)) * u).astype(BF16), wsd_ref[...])
    o = x_ref[...] + g_ref[...] * (routed + shared)
    o_ref[...] = o
    y = _rmsnorm_rows(o, ng_ref[...])
    n_ref[...] = (y * (1.0 + nsc_ref[...]) + nsh_ref[...]).astype(n_ref.dtype)


def moe_combine(rows, yt, h, wsg, wsu, wsd, l, x, gate, next_g, next_sc, next_sh, next_dtype, row0, tm=128):
    n, d = x.shape
    tok_rows = d // 2 // LANES
    yt = yt.reshape(TOP_K + 1, n * tok_rows, LANES)
    s = wsg.shape[2]
    t0 = row0 // tm
    row_map = lambda i: (i + t0, 0)
    mod_map = lambda i: (rows.mod_index((i + t0) * tm), 0, 0)
    return pl.pallas_call(
        functools.partial(_moe_combine_kernel, tok_rows=tok_rows),
        grid=((n - row0) // tm,),
        in_specs=[pl.BlockSpec((TOP_K, tm * tok_rows, LANES), lambda i: (0, i + t0, 0)),
                  pl.BlockSpec((tm, d), row_map),
                  pl.BlockSpec((None, d, s), lambda i: (l, 0, 0)),
                  pl.BlockSpec((None, d, s), lambda i: (l, 0, 0)),
                  pl.BlockSpec((None, s, d), lambda i: (l, 0, 0)),
                  pl.BlockSpec((tm, d), row_map),
                  pl.BlockSpec((None, 1, d), mod_map),
                  pl.BlockSpec((1, d), lambda i: (0, 0)),
                  pl.BlockSpec((None, 1, d), mod_map),
                  pl.BlockSpec((None, 1, d), mod_map)],
        out_specs=[pl.BlockSpec((tm, d), row_map),
                   pl.BlockSpec((tm, d), lambda i: (i, 0))],
        out_shape=[jax.ShapeDtypeStruct(x.shape, x.dtype),
                   jax.ShapeDtypeStruct((n - row0, d), next_dtype)],
        scratch_shapes=[pltpu.VMEM((tm * tok_rows, LANES), F32), pltpu.VMEM((tm * tok_rows, LANES), F32)],
        input_output_aliases={5: 0},
        compiler_params=_params("arbitrary"),
        name="moe_combine",
    )(yt, h, wsg, wsu, wsd, x, gate, next_g.reshape(1, d), next_sc, next_sh)


def kernel(x, c, ctx, c_ctx, g_attn, g_ffn, w_mod_down, w_mod_up, b_mod_up, w_in, q_norm, k_norm, rpb, w_branch_a, w_branch_b, w_out, w_router, b_router, w_exp_gate, w_exp_up, w_exp_down, w_sh_gate, w_sh_up, w_sh_down, g_final):
    batch, seq, d = x.shape
    ctx_len = ctx.shape[1]
    depth = w_in.shape[0]
    n_experts = w_exp_gate.shape[1]
    rows = Rows(batch, ctx_len, seq)
    assert rows.nc % rows.seq == 0 and batch < MOD_ROWS

    a_q, a_kv, b_w = A_HEADS * HEAD_DIM, A_KV_HEADS * HEAD_DIM, B_HEADS * HEAD_DIM
    col_qa = 0
    col_qb = col_qa + a_q
    col_ka = col_qb + b_w
    col_va = col_ka + a_kv
    col_kb = col_va + a_kv
    col_vb = col_kb + b_w
    col_ga = col_vb + b_w
    col_gb = col_ga + d

    cvec = jnp.zeros((MOD_ROWS, d), F32).at[:batch].set(c).at[batch].set(c_ctx)
    mod = modulation_all(cvec, w_mod_down, w_mod_up, b_mod_up)

    w_in_b, w_a_b, w_b_b, w_out_b = (w.astype(BF16) for w in (w_in, w_branch_a, w_branch_b, w_out))
    wsg_b, wsu_b, wsd_b = (w.astype(BF16) for w in (w_sh_gate, w_sh_up, w_sh_down))

    prep_tm = 256
    cos, sin = rope_tables(rows, prep_tm)
    xs = jnp.concatenate([ctx.reshape(rows.nc, d), x.reshape(rows.nx, d)], axis=0)

    mods = [[mod[l, :batch + 1, k * d:(k + 1) * d][:, None, :] for k in range(N_MOD)] for l in range(depth)]
    no_mod = jnp.zeros((batch + 1, 1, d), F32)
    h1 = norm_mod(rows, xs, g_attn[0], mods[0][1], mods[0][0], 0)
    for l in range(depth):
        update_ctx = l < depth - 1
        row0 = 0 if update_ctx else rows.nc
        sh1, sc1, gt1, sh2, sc2, gt2 = mods[l]

        px = matmul(h1, w_in_b, l)
        qa, ka = qk_prep(rows, px, (col_qa, a_q), (col_ka, a_kv), q_norm[l], k_norm[l], cos, sin, prep_tm)
        oa = attn_a(rows, qa, ka, px, col_va)
        ob = attn_b(rows, px, col_qb, col_kb, col_vb, nb_bias(rpb[l], seq // GRID_W))
        if update_ctx:
            oa = attn_ctx(rows, qa, 0, ka, 0, px, col_va, oa, A_HEADS, A_KV_HEADS)
            ob = attn_ctx(rows, px, col_qb, px, col_kb, px, col_vb, ob, B_HEADS, B_HEADS)
        y = merge(oa, ob, w_a_b, w_b_b, l, px, col_ga, col_gb, row0)
        xs = resid_matmul(rows, y, w_out_b, l, xs, gt1, row0, tm=1024)

        h2, htok, idx_t, wgt_t = norm_router(rows, xs, g_ffn[l], sc2, sh2, w_router[l], b_router[l], row0)
        te, used_steps, blk, wrow = moe_plan(idx_t[row0:, :TOP_K], wgt_t[row0:, :TOP_K], row0, n_experts, rows.n, d // 2 // LANES)
        yt = moe_experts(htok, te, used_steps, blk, wrow, w_exp_gate, w_exp_up, w_exp_down, l, rows.n)
        if update_ctx:
            nxt = (g_attn[l + 1], mods[l + 1][1], mods[l + 1][0], BF16)
        else:
            nxt = (g_final, no_mod, no_mod, F32)
        xs, h1 = moe_combine(rows, yt, h2, wsg_b, wsu_b, wsd_b, l, xs, gt2, *nxt, row0)

    return h1.reshape(batch, seq, d)
```

```python
import functools
import math

import jax
import jax.numpy as jnp
from jax import lax
from jax.experimental import pallas as pl
from jax.experimental.pallas import tpu as pltpu

GRID_W = 64
HEAD_DIM = 128
A_HEADS = 16
A_KV_HEADS = 4
B_HEADS = 16
NA_KH_MAX = 8
NA_KW = 16
ROPE_THETA = 10000.0
N_MOD = 6
TOP_K = 8
N_GROUPS = 8
TOPK_GROUPS = 4
ROUTED_SCALE = 2.5
EPS = 1e-6

F32 = jnp.float32
BF16 = jnp.bfloat16
U32 = jnp.uint32
HIGHEST = lax.Precision.HIGHEST
NEG_INF = float("-inf")
LANES = 128

VMEM_LIMIT_BYTES = 56 * 2**20
MOD_ROWS = 16
MOE_TM = 256


def _params(*sem):
    return pltpu.CompilerParams(dimension_semantics=sem, vmem_limit_bytes=VMEM_LIMIT_BYTES)


def _dot(a, b):
    return jnp.dot(a, b, preferred_element_type=F32)


def _dot_nt(a, b):
    return lax.dot_general(a, b, (((1,), (1,)), ((), ())), preferred_element_type=F32)


def _sigmoid(x):
    return 1.0 / (1.0 + jnp.exp(-x))


def _mod_down_kernel(c_ref, w_ref, o_ref):
    c = c_ref[...]
    o_ref[...] = jnp.dot(c * _sigmoid(c), w_ref[...], preferred_element_type=F32, precision=HIGHEST)


def _mod_up_kernel(t_ref, w_ref, b_ref, o_ref):
    o_ref[...] = jnp.dot(t_ref[...], w_ref[...], preferred_element_type=F32, precision=HIGHEST) + b_ref[...]


def modulation_all(cvec, w_down, w_up, b_up, tn=2048):
    depth, d, r = w_down.shape
    n = w_up.shape[2]
    t = pl.pallas_call(
        _mod_down_kernel,
        grid=(depth,),
        in_specs=[pl.BlockSpec((MOD_ROWS, d), lambda l: (0, 0)),
                  pl.BlockSpec((None, d, r), lambda l: (l, 0, 0))],
        out_specs=pl.BlockSpec((None, MOD_ROWS, r), lambda l: (l, 0, 0)),
        out_shape=jax.ShapeDtypeStruct((depth, MOD_ROWS, r), F32),
        compiler_params=_params("arbitrary"),
        name="mod_down",
    )(cvec, w_down)
    return pl.pallas_call(
        _mod_up_kernel,
        grid=(depth, n // tn),
        in_specs=[pl.BlockSpec((None, MOD_ROWS, r), lambda l, j: (l, 0, 0)),
                  pl.BlockSpec((None, r, tn), lambda l, j: (l, 0, j)),
                  pl.BlockSpec((None, 1, tn), lambda l, j: (l, 0, j))],
        out_specs=pl.BlockSpec((None, MOD_ROWS, tn), lambda l, j: (l, 0, j)),
        out_shape=jax.ShapeDtypeStruct((depth, MOD_ROWS, n), F32),
        compiler_params=_params("arbitrary", "arbitrary"),
        name="mod_up",
    )(t, w_up, b_up.reshape(depth, 1, n))


class Rows:
    def __init__(self, batch, ctx_len, seq):
        self.batch, self.ctx_len, self.seq = batch, ctx_len, seq
        self.nc = batch * ctx_len
        self.nx = batch * seq
        self.n = self.nc + self.nx

    def mod_index(self, row):
        return jnp.where(row < self.nc, self.batch, (row - self.nc) // self.seq)


def _rmsnorm_rows(x, g):
    ms = jnp.mean(x * x, axis=-1, keepdims=True)
    return x * lax.rsqrt(ms + EPS) * g


def _norm_mod_kernel(x_ref, g_ref, sc_ref, sh_ref, o_ref):
    y = _rmsnorm_rows(x_ref[...], g_ref[...])
    o_ref[...] = (y * (1.0 + sc_ref[...]) + sh_ref[...]).astype(o_ref.dtype)


def norm_mod(rows, x, g, sc, sh, row0, tm=256):
    n, d = x.shape
    t0 = row0 // tm
    mod_map = lambda i: (rows.mod_index((i + t0) * tm), 0, 0)
    return pl.pallas_call(
        _norm_mod_kernel,
        grid=((n - row0) // tm,),
        in_specs=[pl.BlockSpec((tm, d), lambda i: (i + t0, 0)),
                  pl.BlockSpec((1, d), lambda i: (0, 0)),
                  pl.BlockSpec((None, 1, d), mod_map),
                  pl.BlockSpec((None, 1, d), mod_map)],
        out_specs=pl.BlockSpec((tm, d), lambda i: (i + t0, 0)),
        out_shape=jax.ShapeDtypeStruct((n, d), BF16),
        compiler_params=_params("arbitrary"),
        name="norm_mod",
    )(x, g.reshape(1, d), sc, sh)


def _mm_kernel(x_ref, w_ref, o_ref):
    o_ref[...] = _dot(x_ref[...], w_ref[...]).astype(o_ref.dtype)


def matmul(x, w, l, tm=1024, tn=1024):
    m, k = x.shape
    n = w.shape[2]
    return pl.pallas_call(
        _mm_kernel,
        grid=(m // tm, n // tn),
        in_specs=[pl.BlockSpec((tm, k), lambda i, j: (i, 0)),
                  pl.BlockSpec((None, k, tn), lambda i, j: (l, 0, j))],
        out_specs=pl.BlockSpec((tm, tn), lambda i, j: (i, j)),
        out_shape=jax.ShapeDtypeStruct((m, n), BF16),
        compiler_params=_params("arbitrary", "arbitrary"),
        name="matmul",
    )(x, w)


def rope_tables(rows, tm):
    half = HEAD_DIM // 2
    quarter = half // 2
    pos = jnp.arange(rows.seq)
    inv_freq = ROPE_THETA ** (-jnp.arange(quarter, dtype=F32) / quarter)
    ang_r = (pos // GRID_W).astype(F32)[:, None] * inv_freq[None, :]
    ang_c = (pos % GRID_W).astype(F32)[:, None] * inv_freq[None, :]
    ang = jnp.concatenate([ang_r, ang_r, ang_c, ang_c], axis=-1)
    sign = jnp.concatenate([-jnp.ones(quarter), jnp.ones(quarter)] * 2).astype(F32)
    cos = jnp.concatenate([jnp.ones((tm, HEAD_DIM), F32), jnp.cos(ang)], axis=0)
    sin = jnp.concatenate([jnp.zeros((tm, HEAD_DIM), F32), jnp.sin(ang) * sign[None, :]], axis=0)
    return cos, sin


def _qk_prep_kernel(q_ref, k_ref, cos_ref, sin_ref, qg_ref, kg_ref, qo_ref, ko_ref):
    cos = cos_ref[...]
    sin = sin_ref[...]
    quarter = HEAD_DIM // 4
    lane = lax.broadcasted_iota(jnp.int32, cos.shape, 1)
    first = (lane & quarter) == 0

    def prep(x_ref, g, o_ref):
        for h in range(x_ref.shape[1] // HEAD_DIM):
            sl = slice(h * HEAD_DIM, (h + 1) * HEAD_DIM)
            y = _rmsnorm_rows(x_ref[:, sl].astype(F32), g)
            partner = jnp.where(first, pltpu.roll(y, HEAD_DIM - quarter, 1), pltpu.roll(y, quarter, 1))
            o_ref[:, sl] = (y * cos + partner * sin).astype(o_ref.dtype)

    prep(q_ref, qg_ref[...], qo_ref)
    prep(k_ref, kg_ref[...], ko_ref)


def qk_prep(rows, px, q_cols, k_cols, q_norm, k_norm, cos, sin, tm=256):
    n = px.shape[0]
    (q0, qw), (k0, kw) = q_cols, k_cols
    per_seq = rows.seq // tm

    def tab_map(i):
        r = i * tm
        return (jnp.where(r < rows.nc, 0, 1 + ((r - rows.nc) // tm) % per_seq), 0)

    return pl.pallas_call(
        _qk_prep_kernel,
        grid=(n // tm,),
        in_specs=[pl.BlockSpec((tm, qw), lambda i: (i, q0 // qw)),
                  pl.BlockSpec((tm, kw), lambda i: (i, k0 // kw)),
                  pl.BlockSpec((tm, HEAD_DIM), tab_map),
                  pl.BlockSpec((tm, HEAD_DIM), tab_map),
                  pl.BlockSpec((1, HEAD_DIM), lambda i: (0, 0)),
                  pl.BlockSpec((1, HEAD_DIM), lambda i: (0, 0))],
        out_specs=[pl.BlockSpec((tm, qw), lambda i: (i, 0)),
                   pl.BlockSpec((tm, kw), lambda i: (i, 0))],
        out_shape=[jax.ShapeDtypeStruct((n, qw), BF16), jax.ShapeDtypeStruct((n, kw), BF16)],
        compiler_params=_params("arbitrary"),
        name="qk_prep",
    )(px, px, cos, sin, q_norm.reshape(1, HEAD_DIM), k_norm.reshape(1, HEAD_DIM))


def _softmax_pv(scores, values, scale):
    c = scale * math.log2(math.e)
    m = functools.reduce(jnp.maximum, [jnp.max(s, axis=-1, keepdims=True) for s in scores])
    ps = [jnp.exp2((s - m) * c) for s in scores]
    l = functools.reduce(jnp.add, [jnp.sum(p, axis=-1, keepdims=True) for p in ps])
    o = functools.reduce(jnp.add, [_dot(p.astype(v.dtype), v) for p, v in zip(ps, values)])
    return o / l


def _attn_ctx_kernel(q_ref, k_ref, v_ref, prev_ref, o_ref, *, nq, nk, scale):
    del prev_ref
    for h in range(nq):
        kh = h * nk // nq
        ks = slice(kh * HEAD_DIM, (kh + 1) * HEAD_DIM)
        qs = slice(h * HEAD_DIM, (h + 1) * HEAD_DIM)
        s = _dot_nt(q_ref[:, qs], k_ref[:, ks])
        o_ref[:, qs] = _softmax_pv([s], [v_ref[:, ks]], scale).astype(o_ref.dtype)


def attn_ctx(rows, q_arr, q0, k_arr, k0, v_arr, v0, out_prev, n_heads, n_kv_heads, nq=4):
    nk = nq * n_kv_heads // n_heads
    qw, kw = nq * HEAD_DIM, nk * HEAD_DIM
    cl = rows.ctx_len
    return pl.pallas_call(
        functools.partial(_attn_ctx_kernel, nq=nq, nk=nk, scale=HEAD_DIM ** -0.5),
        grid=(rows.batch, n_heads // nq),
        in_specs=[pl.BlockSpec((cl, qw), lambda b, g: (b, q0 // qw + g)),
                  pl.BlockSpec((cl, kw), lambda b, g: (b, k0 // kw + g)),
                  pl.BlockSpec((cl, kw), lambda b, g: (b, v0 // kw + g)),
                  pl.BlockSpec(memory_space=pl.ANY)],
        out_specs=pl.BlockSpec((cl, qw), lambda b, g: (b, g)),
        out_shape=jax.ShapeDtypeStruct(out_prev.shape, out_prev.dtype),
        input_output_aliases={3: 0},
        compiler_params=_params("arbitrary", "arbitrary"),
        name="attn_ctx",
    )(q_arr, k_arr, v_arr, out_prev)


def _attn_a_kernel(q_ref, kc_ref, kl_ref, vc_ref, vl_ref, o_ref, *, nq, scale):
    kc, kl, vc, vl = kc_ref[...], kl_ref[...], vc_ref[...], vl_ref[...]
    for h in range(nq):
        qs = slice(h * HEAD_DIM, (h + 1) * HEAD_DIM)
        q = q_ref[:, qs]
        o = _softmax_pv([_dot_nt(q, kc), _dot_nt(q, kl)], [vc, vl], scale)
        o_ref[:, qs] = o.astype(o_ref.dtype)


def attn_a(rows, qa, ka, px, v0, tq=512):
    nq = A_HEADS // A_KV_HEADS
    qw = nq * HEAD_DIM
    n_q_blocks = rows.seq // tq
    lat0 = rows.nc // rows.seq
    return pl.pallas_call(
        functools.partial(_attn_a_kernel, nq=nq, scale=HEAD_DIM ** -0.5),
        grid=(rows.batch, A_KV_HEADS, n_q_blocks),
        in_specs=[pl.BlockSpec((tq, qw), lambda b, g, i: (rows.nc // tq + b * n_q_blocks + i, g)),
                  pl.BlockSpec((rows.ctx_len, HEAD_DIM), lambda b, g, i: (b, g)),
                  pl.BlockSpec((rows.seq, HEAD_DIM), lambda b, g, i: (lat0 + b, g)),
                  pl.BlockSpec((rows.ctx_len, HEAD_DIM), lambda b, g, i: (b, v0 // HEAD_DIM + g)),
                  pl.BlockSpec((rows.seq, HEAD_DIM), lambda b, g, i: (lat0 + b, v0 // HEAD_DIM + g))],
        out_specs=pl.BlockSpec((tq, qw), lambda b, g, i: (rows.nc // tq + b * n_q_blocks + i, g)),
        out_shape=jax.ShapeDtypeStruct((rows.n, A_HEADS * HEAD_DIM), BF16),
        compiler_params=_params("arbitrary", "arbitrary", "arbitrary"),
        name="attn_a",
    )(qa, ka, ka, px, px)


NB_QROWS = 8
NB_KROWS = 16


def _nb_slab_start(i, grid_rows):
    lo = i * NB_QROWS - NA_KH_MAX // 2
    hi = grid_rows - NB_KROWS
    if isinstance(i, int):
        return min(max(lo, 0), hi)
    return jnp.clip(lo, 0, hi)


def _nb_bias_kernel(rpb_ref, o_ref, *, grid_rows, inv_scale):
    h = pl.program_id(0)
    w = GRID_W
    rpb_h, rpb_w = 2 * NA_KH_MAX - 1, 2 * NA_KW - 1
    qc = lax.broadcasted_iota(jnp.int32, (w, w), 0)
    kc = lax.broadcasted_iota(jnp.int32, (w, w), 1)
    cs = jnp.clip(qc - NA_KW // 2, 0, w - NA_KW)
    col_ok = (kc >= cs) & (kc < cs + NA_KW)
    dc = kc - qc + NA_KW - 1
    dc_masks = [dc == j for j in range(rpb_w)]
    tiles = []
    for dr in range(rpb_h):
        t = jnp.zeros((w, w), F32)
        for j in range(rpb_w):
            t = jnp.where(dc_masks[j], rpb_ref[(h * rpb_h + dr) * rpb_w + j], t)
        tiles.append(jnp.where(col_ok, t * inv_scale, NEG_INF))
    blank = jnp.full((w, w), NEG_INF, F32)
    for i in range(grid_rows // NB_QROWS):
        k0 = _nb_slab_start(i, grid_rows)
        for qr in range(NB_QROWS):
            r = i * NB_QROWS + qr
            rs = min(max(r - NA_KH_MAX // 2, 0), grid_rows - NA_KH_MAX)
            row = []
            for kr in range(NB_KROWS):
                key_row = k0 + kr
                ok = rs <= key_row < rs + NA_KH_MAX
                row.append(tiles[key_row - r + NA_KH_MAX - 1] if ok else blank)
            o_ref[i, qr * w:(qr + 1) * w, :] = jnp.concatenate(row, axis=1)


def nb_bias(rpb, grid_rows):
    n_heads = rpb.shape[0]
    nqb = grid_rows // NB_QROWS
    tq, tk = NB_QROWS * GRID_W, NB_KROWS * GRID_W
    return pl.pallas_call(
        functools.partial(_nb_bias_kernel, grid_rows=grid_rows, inv_scale=HEAD_DIM ** 0.5),
        grid=(n_heads,),
        in_specs=[pl.BlockSpec(memory_space=pltpu.SMEM)],
        out_specs=pl.BlockSpec((None, nqb, tq, tk), lambda h: (h, 0, 0, 0)),
        out_shape=jax.ShapeDtypeStruct((n_heads, nqb, tq, tk), F32),
        compiler_params=_params("arbitrary"),
        name="nb_bias",
    )(rpb.reshape(-1))


def _attn_b_kernel(q_ref, kc_ref, kl_ref, vc_ref, vl_ref, bias_ref, o_ref, *, nh, scale, grid_rows):
    i = pl.program_id(1)
    tk = NB_KROWS * GRID_W
    start = pl.multiple_of(_nb_slab_start(i, grid_rows) * GRID_W, GRID_W)
    for h in range(nh):
        hs = slice(h * HEAD_DIM, (h + 1) * HEAD_DIM)
        q = q_ref[:, hs]
        s_c = _dot_nt(q, kc_ref[:, hs])
        s_w = _dot_nt(q, kl_ref[pl.ds(start, tk), hs]) + bias_ref[h]
        o = _softmax_pv([s_c, s_w], [vc_ref[:, hs], vl_ref[pl.ds(start, tk), hs]], scale)
        o_ref[:, hs] = o.astype(o_ref.dtype)


def attn_b(rows, px, q0, k0, v0, bias, nh=2):
    grid_rows = rows.seq // GRID_W
    tq = NB_QROWS * GRID_W
    nqb = rows.seq // tq
    hw = nh * HEAD_DIM
    lat0 = rows.nc // rows.seq
    q_map = lambda g, i, b: (rows.nc // tq + b * nqb + i, q0 // hw + g)
    return pl.pallas_call(
        functools.partial(_attn_b_kernel, nh=nh, scale=HEAD_DIM ** -0.5, grid_rows=grid_rows),
        grid=(B_HEADS // nh, nqb, rows.batch),
        in_specs=[pl.BlockSpec((tq, hw), q_map),
                  pl.BlockSpec((rows.ctx_len, hw), lambda g, i, b: (b, k0 // hw + g)),
                  pl.BlockSpec((rows.seq, hw), lambda g, i, b: (lat0 + b, k0 // hw + g)),
                  pl.BlockSpec((rows.ctx_len, hw), lambda g, i, b: (b, v0 // hw + g)),
                  pl.BlockSpec((rows.seq, hw), lambda g, i, b: (lat0 + b, v0 // hw + g)),
                  pl.BlockSpec((nh, None, tq, NB_KROWS * GRID_W), lambda g, i, b: (g, i, 0, 0))],
        out_specs=pl.BlockSpec((tq, hw), lambda g, i, b: (rows.nc // tq + b * nqb + i, g)),
        out_shape=jax.ShapeDtypeStruct((rows.n, B_HEADS * HEAD_DIM), BF16),
        compiler_params=_params("arbitrary", "arbitrary", "arbitrary"),
        name="attn_b",
    )(px, px, px, px, px, bias)


def _merge_kernel(oa_ref, ob_ref, wa_ref, wb_ref, ga_ref, gb_ref, o_ref):
    a = _dot(oa_ref[...], wa_ref[...])
    b = _dot(ob_ref[...], wb_ref[...])
    y = _sigmoid(ga_ref[...].astype(F32)) * a + _sigmoid(gb_ref[...].astype(F32)) * b
    o_ref[...] = y.astype(o_ref.dtype)


def merge(oa, ob, wa, wb, l, px, ga0, gb0, row0, tm=1024, tn=1024):
    n, k = oa.shape
    d = wa.shape[2]
    t0 = row0 // tm
    return pl.pallas_call(
        _merge_kernel,
        grid=((n - row0) // tm, d // tn),
        in_specs=[pl.BlockSpec((tm, k), lambda i, j: (i + t0, 0)),
                  pl.BlockSpec((tm, k), lambda i, j: (i + t0, 0)),
                  pl.BlockSpec((None, k, tn), lambda i, j: (l, 0, j)),
                  pl.BlockSpec((None, k, tn), lambda i, j: (l, 0, j)),
                  pl.BlockSpec((tm, tn), lambda i, j: (i + t0, ga0 // tn + j)),
                  pl.BlockSpec((tm, tn), lambda i, j: (i + t0, gb0 // tn + j))],
        out_specs=pl.BlockSpec((tm, tn), lambda i, j: (i + t0, j)),
        out_shape=jax.ShapeDtypeStruct((n, d), BF16),
        compiler_params=_params("arbitrary", "arbitrary"),
        name="merge",
    )(oa, ob, wa, wb, px, px)


def _resid_mm_kernel(y_ref, w_ref, x_ref, g_ref, o_ref):
    o_ref[...] = x_ref[...] + g_ref[...] * _dot(y_ref[...], w_ref[...])


def resid_matmul(rows, y, w, l, x, gate, row0, tm, tn=1024):
    n, k = y.shape
    d = w.shape[2]
    t0 = row0 // tm
    return pl.pallas_call(
        _resid_mm_kernel,
        grid=((n - row0) // tm, d // tn),
        in_specs=[pl.BlockSpec((tm, k), lambda i, j: (i + t0, 0)),
                  pl.BlockSpec((None, k, tn), lambda i, j: (l, 0, j)),
                  pl.BlockSpec((tm, tn), lambda i, j: (i + t0, j)),
                  pl.BlockSpec((None, 1, tn), lambda i, j: (rows.mod_index((i + t0) * tm), 0, j))],
        out_specs=pl.BlockSpec((tm, tn), lambda i, j: (i + t0, j)),
        out_shape=jax.ShapeDtypeStruct(x.shape, x.dtype),
        input_output_aliases={2: 0},
        compiler_params=_params("arbitrary", "arbitrary"),
        name="resid_matmul",
    )(y, w, x, gate)


def _pack_words(v):
    half = v.shape[1] // 2
    bits = lax.bitcast_convert_type(v.astype(BF16).astype(F32), U32)
    return (bits[:, :half] >> 16) | bits[:, half:]


def _unpack_words(w):
    lo = lax.bitcast_convert_type(w << 16, F32)
    hi = lax.bitcast_convert_type(w & U32(0xFFFF0000), F32)
    return lo, hi


def _store_token_major(ref, words):
    rows, nch = words.shape[0], words.shape[1] // LANES
    for c in range(nch):
        ref[pl.ds(c, rows, stride=nch), :] = words[:, c * LANES:(c + 1) * LANES]


def _load_token_major(ref, rows, nch):
    return [ref[pl.ds(c, rows, stride=nch), :] for c in range(nch)]


def _route(logits, bias, n_experts):
    lane = lax.broadcasted_iota(jnp.int32, logits.shape, 1)
    lane_f = lane.astype(F32)
    per_group = n_experts // N_GROUPS
    valid = lane < n_experts
    scores = _sigmoid(logits)
    choice = jnp.where(valid, scores + bias, NEG_INF)
    big = float(LANES)

    def first_argmax(x):
        m = jnp.max(x, axis=-1, keepdims=True)
        idx = jnp.min(jnp.where(x == m, lane_f, big), axis=-1, keepdims=True)
        return m, idx

    in_group = [(lane >= g * per_group) & (lane < (g + 1) * per_group) for g in range(N_GROUPS)]
    group_score = []
    for g in range(N_GROUPS):
        xg = jnp.where(in_group[g], choice, NEG_INF)
        m1, i1 = first_argmax(xg)
        m2 = jnp.max(jnp.where(lane_f == i1, NEG_INF, xg), axis=-1, keepdims=True)
        group_score.append(m1 + m2)
    ok = jnp.zeros(logits.shape, F32)
    for g in range(N_GROUPS):
        ahead = jnp.zeros(group_score[g].shape, F32)
        for o in range(N_GROUPS):
            if o != g:
                wins = group_score[o] >= group_score[g] if o < g else group_score[o] > group_score[g]
                ahead = ahead + jnp.where(wins, 1.0, 0.0)
        ok = jnp.where(in_group[g], jnp.where(ahead < TOPK_GROUPS, 1.0, 0.0), ok)
    x = jnp.where(ok > 0.5, choice, NEG_INF)
    idxs, ws = [], []
    for _ in range(TOP_K):
        _, idx = first_argmax(x)
        hit = lane_f == idx
        idxs.append(idx)
        ws.append(jnp.sum(jnp.where(hit, scores, 0.0), axis=-1, keepdims=True))
        x = jnp.where(hit, NEG_INF, x)
    total = functools.reduce(jnp.add, ws)
    idx_tile = jnp.zeros(logits.shape, F32)
    w_tile = jnp.zeros(logits.shape, F32)
    for j in range(TOP_K):
        idx_tile = jnp.where(lane == j, idxs[j], idx_tile)
        w_tile = jnp.where(lane == j, ws[j] / total * ROUTED_SCALE, w_tile)
    return idx_tile.astype(jnp.int32), w_tile


def _norm_router_kernel(x_ref, g_ref, sc_ref, sh_ref, wh_ref, wl_ref, br_ref,
                        h_ref, htok_ref, idx_ref, wgt_ref, *, n_experts):
    y = _rmsnorm_rows(x_ref[...], g_ref[...])
    h = y * (1.0 + sc_ref[...]) + sh_ref[...]
    h_hi = h.astype(BF16)
    h_ref[...] = h_hi
    _store_token_major(htok_ref, _pack_words(h))
    h_lo = (h - h_hi.astype(F32)).astype(BF16)
    wh = wh_ref[...]
    logits = _dot(h_hi, wh) + _dot(h_lo, wh) + _dot(h_hi, wl_ref[...])
    idx_ref[...], wgt_ref[...] = _route(logits, br_ref[...], n_experts)


def norm_router(rows, x, g, sc, sh, w_router, b_router, row0, tm=256):
    n, d = x.shape
    e = w_router.shape[1]
    tok_rows = d // 2 // LANES
    t0 = row0 // tm
    wr = jnp.zeros((d, LANES), F32).at[:, :e].set(w_router)
    wr_hi = wr.astype(BF16)
    wr_lo = (wr - wr_hi.astype(F32)).astype(BF16)
    br = jnp.zeros((1, LANES), F32).at[0, :e].set(b_router)
    mod_map = lambda i: (rows.mod_index((i + t0) * tm), 0, 0)
    const = lambda i: (0, 0)
    row_map = lambda i: (i + t0, 0)
    return pl.pallas_call(
        functools.partial(_norm_router_kernel, n_experts=e),
        grid=((n - row0) // tm,),
        in_specs=[pl.BlockSpec((tm, d), row_map),
                  pl.BlockSpec((1, d), const),
                  pl.BlockSpec((None, 1, d), mod_map),
                  pl.BlockSpec((None, 1, d), mod_map),
                  pl.BlockSpec((d, LANES), const),
                  pl.BlockSpec((d, LANES), const),
                  pl.BlockSpec((1, LANES), const)],
        out_specs=[pl.BlockSpec((tm, d), row_map),
                   pl.BlockSpec((tm * tok_rows, LANES), row_map),
                   pl.BlockSpec((tm, LANES), row_map),
                   pl.BlockSpec((tm, LANES), row_map)],
        out_shape=[jax.ShapeDtypeStruct((n, d), BF16),
                   jax.ShapeDtypeStruct((n * tok_rows, LANES), U32),
                   jax.ShapeDtypeStruct((n, LANES), jnp.int32),
                   jax.ShapeDtypeStruct((n, LANES), F32)],
        compiler_params=_params("arbitrary"),
        name="norm_router",
    )(x, g.reshape(1, d), sc, sh, wr_hi, wr_lo, br)


def moe_plan(idx, wgt, tok0, n_experts, n_rows, tok_rows):
    t, p = idx.shape[0], idx.shape[0] * TOP_K
    n_tiles = p // MOE_TM + n_experts
    assert n_tiles % 2 == 0 and p % MOE_TM == 0 and 2 * MOE_TM <= n_rows
    flat_e = idx.reshape(p)
    order = jnp.argsort(flat_e, stable=True).astype(jnp.int32)
    counts = jnp.sum((flat_e[:, None] == jnp.arange(n_experts, dtype=jnp.int32)[None, :]).astype(jnp.int32), axis=0)
    tiles_per = (counts + MOE_TM - 1) // MOE_TM
    tile_end = jnp.cumsum(tiles_per)
    tile_start = tile_end - tiles_per
    pair_start = jnp.cumsum(counts) - counts
    tile = jnp.arange(n_tiles, dtype=jnp.int32)
    te = jnp.sum((tile[:, None] >= tile_end[None, :]).astype(jnp.int32), axis=1)
    te = jnp.minimum(te, n_experts - 1)
    lane = jnp.arange(MOE_TM, dtype=jnp.int32)[None, :]
    r = (tile - tile_start[te])[:, None] * MOE_TM + lane
    valid = (r < counts[te][:, None]) & (tile < tile_end[-1])[:, None]
    pair = order[jnp.clip(pair_start[te][:, None] + r, 0, p - 1)]
    tok = pair // TOP_K + tok0
    plane = pair % TOP_K
    slot = tile[:, None] * MOE_TM + lane
    src = jnp.where(valid, tok, tok0 + slot % t)
    dump = TOP_K * n_rows + (tile % 2)[:, None] * MOE_TM + lane
    dst = jnp.where(valid, plane * n_rows + tok, dump)
    wrow = jnp.where(valid, wgt.reshape(p)[pair], 0.0).reshape(n_tiles * MOE_TM, 1)
    src, dst = src * tok_rows, dst * tok_rows
    pad = jnp.zeros((n_tiles // 2, IDX_ROWS - 2, MOE_TM), jnp.int32)
    blk = jnp.concatenate([jnp.stack([src[0::2], src[1::2]], axis=1), pad,
                           jnp.stack([dst[0::2], dst[1::2]], axis=1), pad], axis=1).astype(jnp.int32)
    used_steps = ((tile_end[-1] + 1) // 2).astype(jnp.int32).reshape(1)
    return te, used_steps, blk, wrow


IDX_ROWS = 8


def _stage_pitch(tok_rows):
    return tok_rows if (tok_rows // 8) % 2 == 1 else tok_rows + 8


def _moe_expert_kernel(te_ref, ns_ref, idx_hbm, htok_hbm, wrow_ref,
                       wga_ref, wua_ref, wda_ref, wgb_ref, wub_ref, wdb_ref,
                       yt_hbm,
                       src_s, dst_s, in_a, in_b, out_a, out_b, idx_sem, in_sem, out_sem,
                       *, n_rows, tok_rows):
    del te_ref
    n_steps = ns_ref[0]
    k = pl.program_id(0)
    tm, nch = MOE_TM, tok_rows
    pitch = _stage_pitch(nch)
    stage_in, stage_out = (in_a, in_b), (out_a, out_b)
    dump0 = TOP_K * n_rows * nch

    def src_copy(step):
        return pltpu.make_async_copy(idx_hbm.at[step, pl.ds(0, IDX_ROWS), :], src_s, idx_sem.at[0])

    def dst_copy(step):
        return pltpu.make_async_copy(idx_hbm.at[step, pl.ds(IDX_ROWS, IDX_ROWS), :], dst_s, idx_sem.at[1])

    def issue_gathers(tile):
        for r in range(tm):
            off = pl.multiple_of(src_s[tile, r], nch)
            pltpu.make_async_copy(htok_hbm.at[pl.ds(off, nch), :],
                                  stage_in[tile].at[pl.ds(r * pitch, nch), :], in_sem.at[tile]).start()

    def wait_gathers(tile):
        pltpu.make_async_copy(htok_hbm.at[pl.ds(0, tm * nch), :], stage_in[tile].at[pl.ds(0, tm * nch), :],
                              in_sem.at[tile]).wait()

    def issue_scatters(tile):
        for r in range(tm):
            off = pl.multiple_of(dst_s[tile, r], nch)
            pltpu.make_async_copy(stage_out[tile].at[pl.ds(r * pitch, nch), :],
                                  yt_hbm.at[pl.ds(off, nch), :], out_sem.at[tile]).start()

    def bulk_scatter(tile):
        return pltpu.make_async_copy(stage_out[tile].at[pl.ds(0, tm * nch), :],
                                     yt_hbm.at[pl.ds(dump0 + tile * tm * nch, tm * nch), :], out_sem.at[tile])

    def load_rows(tile):
        los, his = [], []
        for c in range(nch):
            lo, hi = _unpack_words(stage_in[tile][pl.ds(c, tm, stride=pitch), :])
            los.append(lo.astype(BF16))
            his.append(hi.astype(BF16))
        return jnp.concatenate(los + his, axis=1)

    def compute(tile, wg_ref, wu_ref, wd_ref):
        x = load_rows(tile)
        hdim = wg_ref.shape[1]
        wgu = jnp.concatenate([wg_ref[...].astype(BF16), wu_ref[...].astype(BF16)], axis=1)
        gu = _dot(x, wgu)
        g, u = gu[:, :hdim], gu[:, hdim:]
        hid = (g * _sigmoid(g)) * u * wrow_ref[tile * tm:(tile + 1) * tm, :]
        y = _dot(hid.astype(BF16), wd_ref[...].astype(BF16))
        words = _pack_words(y)
        for c in range(nch):
            stage_out[tile][pl.ds(c, tm, stride=pitch), :] = words[:, c * LANES:(c + 1) * LANES]

    @pl.when(k < n_steps)
    def _():
        last = k == n_steps - 1
        next_step = jnp.where(last, 0, k + 1)

        @pl.when(k == 0)
        def _():
            first = src_copy(0)
            first.start()
            first.wait()
            issue_gathers(0)
            issue_gathers(1)
            src_copy(next_step).start()
            dst_copy(0).start()
            for tile in range(2):
                stage_out[tile][...] = jnp.zeros(stage_out[tile].shape, U32)
                bulk_scatter(tile).start()

        wait_gathers(0)
        bulk_scatter(0).wait()
        compute(0, wga_ref, wua_ref, wda_ref)
        dst_copy(k).wait()
        issue_scatters(0)
        src_copy(next_step).wait()
        issue_gathers(0)

        wait_gathers(1)
        bulk_scatter(1).wait()
        compute(1, wgb_ref, wub_ref, wdb_ref)
        issue_scatters(1)
        issue_gathers(1)

        @pl.when(jnp.logical_not(last))
        def _():
            src_copy(jnp.where(k + 2 < n_steps, k + 2, 0)).start()
            dst_copy(k + 1).start()

        @pl.when(last)
        def _():
            for tile in range(2):
                wait_gathers(tile)
                bulk_scatter(tile).wait()


def moe_experts(htok, te, used_steps, blk, wrow, wg, wu, wd, l, n_rows):
    n_steps = blk.shape[0]
    tok_rows = htok.shape[0] // n_rows
    _, _, d, hdim = wg.shape
    stage = pltpu.VMEM((MOE_TM * _stage_pitch(tok_rows), LANES), U32)
    w_in_spec = lambda off: pl.BlockSpec((None, None, d, hdim), lambda k, te, ns: (l, te[2 * k + off], 0, 0))
    w_out_spec = lambda off: pl.BlockSpec((None, None, hdim, d), lambda k, te, ns: (l, te[2 * k + off], 0, 0))
    return pl.pallas_call(
        functools.partial(_moe_expert_kernel, n_rows=n_rows, tok_rows=tok_rows),
        grid_spec=pltpu.PrefetchScalarGridSpec(
            num_scalar_prefetch=2,
            grid=(n_steps,),
            in_specs=[pl.BlockSpec(memory_space=pl.ANY),
                      pl.BlockSpec(memory_space=pl.ANY),
                      pl.BlockSpec((2 * MOE_TM, 1), lambda k, te, ns: (k, 0)),
                      w_in_spec(0), w_in_spec(0), w_out_spec(0),
                      w_in_spec(1), w_in_spec(1), w_out_spec(1)],
            out_specs=pl.BlockSpec(memory_space=pl.ANY),
            scratch_shapes=[pltpu.SMEM((IDX_ROWS, MOE_TM), jnp.int32),
                            pltpu.SMEM((IDX_ROWS, MOE_TM), jnp.int32),
                            stage, stage, stage, stage,
                            pltpu.SemaphoreType.DMA((2,)),
                            pltpu.SemaphoreType.DMA((2,)),
                            pltpu.SemaphoreType.DMA((2,))],
        ),
        out_shape=jax.ShapeDtypeStruct(((TOP_K + 1) * n_rows * tok_rows, LANES), U32),
        compiler_params=_params("arbitrary"),
        name="moe_experts",
    )(te, used_steps, blk, htok, wrow, wg, wu, wd, wg, wu, wd)


def _moe_combine_kernel(yt_ref, h_ref, wsg_ref, wsu_ref, wsd_ref, x_ref, g_ref, ng_ref, nsc_ref, nsh_ref,
                        o_ref, n_ref, acc_lo, acc_hi, *, tok_rows):
    tm = x_ref.shape[0]
    lo, hi = _unpack_words(yt_ref[0])
    for j in range(1, TOP_K):
        l2, h2 = _unpack_words(yt_ref[j])
        lo, hi = lo + l2, hi + h2
    acc_lo[...] = lo
    acc_hi[...] = hi
    routed = jnp.concatenate(_load_token_major(acc_lo, tm, tok_rows) + _load_token_major(acc_hi, tm, tok_rows), axis=1)
    h = h_ref[...]
    g = _dot(h, wsg_ref[...])
    u = _dot(h, wsu_ref[...])
    shared = _dot(((g * _sigmoid(g)) * u).astype(BF16), wsd_ref[...])
    o = x_ref[...] + g_ref[...] * (routed + shared)
    o_ref[...] = o
    y = _rmsnorm_rows(o, ng_ref[...])
    n_ref[...] = (y * (1.0 + nsc_ref[...]) + nsh_ref[...]).astype(n_ref.dtype)


def moe_combine(rows, yt, h, wsg, wsu, wsd, l, x, gate, next_g, next_sc, next_sh, next_dtype, row0, tm=128):
    n, d = x.shape
    tok_rows = d // 2 // LANES
    yt = yt.reshape(TOP_K + 1, n * tok_rows, LANES)
    s = wsg.shape[2]
    t0 = row0 // tm
    row_map = lambda i: (i + t0, 0)
    mod_map = lambda i: (rows.mod_index((i + t0) * tm), 0, 0)
    return pl.pallas_call(
        functools.partial(_moe_combine_kernel, tok_rows=tok_rows),
        grid=((n - row0) // tm,),
        in_specs=[pl.BlockSpec((TOP_K, tm * tok_rows, LANES), lambda i: (0, i + t0, 0)),
                  pl.BlockSpec((tm, d), row_map),
                  pl.BlockSpec((None, d, s), lambda i: (l, 0, 0)),
                  pl.BlockSpec((None, d, s), lambda i: (l, 0, 0)),
                  pl.BlockSpec((None, s, d), lambda i: (l, 0, 0)),
                  pl.BlockSpec((tm, d), row_map),
                  pl.BlockSpec((None, 1, d), mod_map),
                  pl.BlockSpec((1, d), lambda i: (0, 0)),
                  pl.BlockSpec((None, 1, d), mod_map),
                  pl.BlockSpec((None, 1, d), mod_map)],
        out_specs=[pl.BlockSpec((tm, d), row_map),
                   pl.BlockSpec((tm, d), lambda i: (i, 0))],
        out_shape=[jax.ShapeDtypeStruct(x.shape, x.dtype),
                   jax.ShapeDtypeStruct((n - row0, d), next_dtype)],
        scratch_shapes=[pltpu.VMEM((tm * tok_rows, LANES), F32), pltpu.VMEM((tm * tok_rows, LANES), F32)],
        input_output_aliases={5: 0},
        compiler_params=_params("arbitrary"),
        name="moe_combine",
    )(yt, h, wsg, wsu, wsd, x, gate, next_g.reshape(1, d), next_sc, next_sh)


def kernel(x, c, ctx, c_ctx, g_attn, g_ffn, w_mod_down, w_mod_up, b_mod_up, w_in, q_norm, k_norm, rpb, w_branch_a, w_branch_b, w_out, w_router, b_router, w_exp_gate, w_exp_up, w_exp_down, w_sh_gate, w_sh_up, w_sh_down, g_final):
    batch, seq, d = x.shape
    ctx_len = ctx.shape[1]
    depth = w_in.shape[0]
    n_experts = w_exp_gate.shape[1]
    rows = Rows(batch, ctx_len, seq)
    assert rows.nc % rows.seq == 0 and batch < MOD_ROWS

    a_q, a_kv, b_w = A_HEADS * HEAD_DIM, A_KV_HEADS * HEAD_DIM, B_HEADS * HEAD_DIM
    col_qa = 0
    col_qb = col_qa + a_q
    col_ka = col_qb + b_w
    col_va = col_ka + a_kv
    col_kb = col_va + a_kv
    col_vb = col_kb + b_w
    col_ga = col_vb + b_w
    col_gb = col_ga + d

    cvec = jnp.zeros((MOD_ROWS, d), F32).at[:batch].set(c).at[batch].set(c_ctx)
    mod = modulation_all(cvec, w_mod_down, w_mod_up, b_mod_up)

    w_in_b, w_a_b, w_b_b, w_out_b = (w.astype(BF16) for w in (w_in, w_branch_a, w_branch_b, w_out))
    wsg_b, wsu_b, wsd_b = (w.astype(BF16) for w in (w_sh_gate, w_sh_up, w_sh_down))

    prep_tm = 256
    cos, sin = rope_tables(rows, prep_tm)
    xs = jnp.concatenate([ctx.reshape(rows.nc, d), x.reshape(rows.nx, d)], axis=0)

    mods = [[mod[l, :batch + 1, k * d:(k + 1) * d][:, None, :] for k in range(N_MOD)] for l in range(depth)]
    no_mod = jnp.zeros((batch + 1, 1, d), F32)
    h1 = norm_mod(rows, xs, g_attn[0], mods[0][1], mods[0][0], 0)
    for l in range(depth):
        update_ctx = l < depth - 1
        row0 = 0 if update_ctx else rows.nc
        sh1, sc1, gt1, sh2, sc2, gt2 = mods[l]

        px = matmul(h1, w_in_b, l)
        qa, ka = qk_prep(rows, px, (col_qa, a_q), (col_ka, a_kv), q_norm[l], k_norm[l], cos, sin, prep_tm)
        oa = attn_a(rows, qa, ka, px, col_va)
        ob = attn_b(rows, px, col_qb, col_kb, col_vb, nb_bias(rpb[l], seq // GRID_W))
        if update_ctx:
            oa = attn_ctx(rows, qa, 0, ka, 0, px, col_va, oa, A_HEADS, A_KV_HEADS)
            ob = attn_ctx(rows, px, col_qb, px, col_kb, px, col_vb, ob, B_HEADS, B_HEADS)
        y = merge(oa, ob, w_a_b, w_b_b, l, px, col_ga, col_gb, row0)
        xs = resid_matmul(rows, y, w_out_b, l, xs, gt1, row0, tm=1024)

        h2, htok, idx_t, wgt_t = norm_router(rows, xs, g_ffn[l], sc2, sh2, w_router[l], b_router[l], row0)
        te, used_steps, blk, wrow = moe_plan(idx_t[row0:, :TOP_K], wgt_t[row0:, :TOP_K], row0, n_experts, rows.n, d // 2 // LANES)
        yt = moe_experts(htok, te, used_steps, blk, wrow, w_exp_gate, w_exp_up, w_exp_down, l, rows.n)
        if update_ctx:
            nxt = (g_attn[l + 1], mods[l + 1][1], mods[l + 1][0], BF16)
        else:
            nxt = (g_final, no_mod, no_mod, F32)
        xs, h1 = moe_combine(rows, yt, h2, wsg_b, wsu_b, wsd_b, l, xs, gt2, *nxt, row0)

    return h1.reshape(batch, seq, d)
```

```python
import functools
import math

import jax
import jax.numpy as jnp
from jax import lax
from jax.experimental import pallas as pl
from jax.experimental.pallas import tpu as pltpu

GRID_W = 64
HEAD_DIM = 128
A_HEADS = 16
A_KV_HEADS = 4
B_HEADS = 16
NA_KH_MAX = 8
NA_KW = 16
ROPE_THETA = 10000.0
N_MOD = 6
TOP_K = 8
N_GROUPS = 8
TOPK_GROUPS = 4
ROUTED_SCALE = 2.5
EPS = 1e-6

F32 = jnp.float32
BF16 = jnp.bfloat16
U32 = jnp.uint32
HIGHEST = lax.Precision.HIGHEST
NEG_INF = float("-inf")
LANES = 128

VMEM_LIMIT_BYTES = 56 * 2**20
MOD_ROWS = 16
MOE_TM = 256


def _params(*sem):
    return pltpu.CompilerParams(dimension_semantics=sem, vmem_limit_bytes=VMEM_LIMIT_BYTES)


def _dot(a, b):
    return jnp.dot(a, b, preferred_element_type=F32)


def _dot_nt(a, b):
    return lax.dot_general(a, b, (((1,), (1,)), ((), ())), preferred_element_type=F32)


def _sigmoid(x):
    return 1.0 / (1.0 + jnp.exp(-x))


def _mod_down_kernel(c_ref, w_ref, o_ref):
    c = c_ref[...]
    o_ref[...] = jnp.dot(c * _sigmoid(c), w_ref[...], preferred_element_type=F32, precision=HIGHEST)


def _mod_up_kernel(t_ref, w_ref, b_ref, o_ref):
    o_ref[...] = jnp.dot(t_ref[...], w_ref[...], preferred_element_type=F32, precision=HIGHEST) + b_ref[...]


def modulation_all(cvec, w_down, w_up, b_up, tn=2048):
    depth, d, r = w_down.shape
    n = w_up.shape[2]
    t = pl.pallas_call(
        _mod_down_kernel,
        grid=(depth,),
        in_specs=[pl.BlockSpec((MOD_ROWS, d), lambda l: (0, 0)),
                  pl.BlockSpec((None, d, r), lambda l: (l, 0, 0))],
        out_specs=pl.BlockSpec((None, MOD_ROWS, r), lambda l: (l, 0, 0)),
        out_shape=jax.ShapeDtypeStruct((depth, MOD_ROWS, r), F32),
        compiler_params=_params("arbitrary"),
        name="mod_down",
    )(cvec, w_down)
    return pl.pallas_call(
        _mod_up_kernel,
        grid=(depth, n // tn),
        in_specs=[pl.BlockSpec((None, MOD_ROWS, r), lambda l, j: (l, 0, 0)),
                  pl.BlockSpec((None, r, tn), lambda l, j: (l, 0, j)),
                  pl.BlockSpec((None, 1, tn), lambda l, j: (l, 0, j))],
        out_specs=pl.BlockSpec((None, MOD_ROWS, tn), lambda l, j: (l, 0, j)),
        out_shape=jax.ShapeDtypeStruct((depth, MOD_ROWS, n), F32),
        compiler_params=_params("arbitrary", "arbitrary"),
        name="mod_up",
    )(t, w_up, b_up.reshape(depth, 1, n))


class Rows:
    def __init__(self, batch, ctx_len, seq):
        self.batch, self.ctx_len, self.seq = batch, ctx_len, seq
        self.nc = batch * ctx_len
        self.nx = batch * seq
        self.n = self.nc + self.nx

    def mod_index(self, row):
        return jnp.where(row < self.nc, self.batch, (row - self.nc) // self.seq)


def _rmsnorm_rows(x, g):
    ms = jnp.mean(x * x, axis=-1, keepdims=True)
    return x * lax.rsqrt(ms + EPS) * g


def _norm_mod_kernel(x_ref, g_ref, sc_ref, sh_ref, o_ref):
    y = _rmsnorm_rows(x_ref[...], g_ref[...])
    o_ref[...] = (y * (1.0 + sc_ref[...]) + sh_ref[...]).astype(o_ref.dtype)


def norm_mod(rows, x, g, sc, sh, row0, tm=256):
    n, d = x.shape
    t0 = row0 // tm
    mod_map = lambda i: (rows.mod_index((i + t0) * tm), 0, 0)
    return pl.pallas_call(
        _norm_mod_kernel,
        grid=((n - row0) // tm,),
        in_specs=[pl.BlockSpec((tm, d), lambda i: (i + t0, 0)),
                  pl.BlockSpec((1, d), lambda i: (0, 0)),
                  pl.BlockSpec((None, 1, d), mod_map),
                  pl.BlockSpec((None, 1, d), mod_map)],
        out_specs=pl.BlockSpec((tm, d), lambda i: (i + t0, 0)),
        out_shape=jax.ShapeDtypeStruct((n, d), BF16),
        compiler_params=_params("arbitrary"),
        name="norm_mod",
    )(x, g.reshape(1, d), sc, sh)


def _mm_kernel(x_ref, w_ref, o_ref):
    o_ref[...] = _dot(x_ref[...], w_ref[...]).astype(o_ref.dtype)


def matmul(x, w, l, tm=1024, tn=1024):
    m, k = x.shape
    n = w.shape[2]
    return pl.pallas_call(
        _mm_kernel,
        grid=(m // tm, n // tn),
        in_specs=[pl.BlockSpec((tm, k), lambda i, j: (i, 0)),
                  pl.BlockSpec((None, k, tn), lambda i, j: (l, 0, j))],
        out_specs=pl.BlockSpec((tm, tn), lambda i, j: (i, j)),
        out_shape=jax.ShapeDtypeStruct((m, n), BF16),
        compiler_params=_params("arbitrary", "arbitrary"),
        name="matmul",
    )(x, w)


def rope_tables(rows, tm):
    half = HEAD_DIM // 2
    quarter = half // 2
    pos = jnp.arange(rows.seq)
    inv_freq = ROPE_THETA ** (-jnp.arange(quarter, dtype=F32) / quarter)
    ang_r = (pos // GRID_W).astype(F32)[:, None] * inv_freq[None, :]
    ang_c = (pos % GRID_W).astype(F32)[:, None] * inv_freq[None, :]
    ang = jnp.concatenate([ang_r, ang_r, ang_c, ang_c], axis=-1)
    sign = jnp.concatenate([-jnp.ones(quarter), jnp.ones(quarter)] * 2).astype(F32)
    cos = jnp.concatenate([jnp.ones((tm, HEAD_DIM), F32), jnp.cos(ang)], axis=0)
    sin = jnp.concatenate([jnp.zeros((tm, HEAD_DIM), F32), jnp.sin(ang) * sign[None, :]], axis=0)
    return cos, sin


def _qk_prep_kernel(q_ref, k_ref, cos_ref, sin_ref, qg_ref, kg_ref, qo_ref, ko_ref):
    cos = cos_ref[...]
    sin = sin_ref[...]
    quarter = HEAD_DIM // 4
    lane = lax.broadcasted_iota(jnp.int32, cos.shape, 1)
    first = (lane & quarter) == 0

    def prep(x_ref, g, o_ref):
        for h in range(x_ref.shape[1] // HEAD_DIM):
            sl = slice(h * HEAD_DIM, (h + 1) * HEAD_DIM)
            y = _rmsnorm_rows(x_ref[:, sl].astype(F32), g)
            partner = jnp.where(first, pltpu.roll(y, HEAD_DIM - quarter, 1), pltpu.roll(y, quarter, 1))
            o_ref[:, sl] = (y * cos + partner * sin).astype(o_ref.dtype)

    prep(q_ref, qg_ref[...], qo_ref)
    prep(k_ref, kg_ref[...], ko_ref)


def qk_prep(rows, px, q_cols, k_cols, q_norm, k_norm, cos, sin, tm=256):
    n = px.shape[0]
    (q0, qw), (k0, kw) = q_cols, k_cols
    per_seq = rows.seq // tm

    def tab_map(i):
        r = i * tm
        return (jnp.where(r < rows.nc, 0, 1 + ((r - rows.nc) // tm) % per_seq), 0)

    return pl.pallas_call(
        _qk_prep_kernel,
        grid=(n // tm,),
        in_specs=[pl.BlockSpec((tm, qw), lambda i: (i, q0 // qw)),
                  pl.BlockSpec((tm, kw), lambda i: (i, k0 // kw)),
                  pl.BlockSpec((tm, HEAD_DIM), tab_map),
                  pl.BlockSpec((tm, HEAD_DIM), tab_map),
                  pl.BlockSpec((1, HEAD_DIM), lambda i: (0, 0)),
                  pl.BlockSpec((1, HEAD_DIM), lambda i: (0, 0))],
        out_specs=[pl.BlockSpec((tm, qw), lambda i: (i, 0)),
                   pl.BlockSpec((tm, kw), lambda i: (i, 0))],
        out_shape=[jax.ShapeDtypeStruct((n, qw), BF16), jax.ShapeDtypeStruct((n, kw), BF16)],
        compiler_params=_params("arbitrary"),
        name="qk_prep",
    )(px, px, cos, sin, q_norm.reshape(1, HEAD_DIM), k_norm.reshape(1, HEAD_DIM))


def _with_ones(v):
    return jnp.concatenate([v, jnp.ones_like(v)], axis=1)


def _softmax_pv(scores, values, scale):
    c = scale * math.log2(math.e)
    d = values[0].shape[1] // 2
    m = functools.reduce(jnp.maximum, [jnp.max(s, axis=-1, keepdims=True) for s in scores])
    o = functools.reduce(jnp.add, [_dot(jnp.exp2((s - m) * c).astype(v.dtype), v) for s, v in zip(scores, values)])
    return o[:, :d] / o[:, d:d + 1]


def _attn_ctx_kernel(q_ref, k_ref, v_ref, prev_ref, o_ref, *, nq, nk, scale):
    del prev_ref
    for h in range(nq):
        kh = h * nk // nq
        ks = slice(kh * HEAD_DIM, (kh + 1) * HEAD_DIM)
        qs = slice(h * HEAD_DIM, (h + 1) * HEAD_DIM)
        s = _dot_nt(q_ref[:, qs], k_ref[:, ks])
        o_ref[:, qs] = _softmax_pv([s], [_with_ones(v_ref[:, ks])], scale).astype(o_ref.dtype)


def attn_ctx(rows, q_arr, q0, k_arr, k0, v_arr, v0, out_prev, n_heads, n_kv_heads, nq=4):
    nk = nq * n_kv_heads // n_heads
    qw, kw = nq * HEAD_DIM, nk * HEAD_DIM
    cl = rows.ctx_len
    return pl.pallas_call(
        functools.partial(_attn_ctx_kernel, nq=nq, nk=nk, scale=HEAD_DIM ** -0.5),
        grid=(rows.batch, n_heads // nq),
        in_specs=[pl.BlockSpec((cl, qw), lambda b, g: (b, q0 // qw + g)),
                  pl.BlockSpec((cl, kw), lambda b, g: (b, k0 // kw + g)),
                  pl.BlockSpec((cl, kw), lambda b, g: (b, v0 // kw + g)),
                  pl.BlockSpec(memory_space=pl.ANY)],
        out_specs=pl.BlockSpec((cl, qw), lambda b, g: (b, g)),
        out_shape=jax.ShapeDtypeStruct(out_prev.shape, out_prev.dtype),
        input_output_aliases={3: 0},
        compiler_params=_params("arbitrary", "arbitrary"),
        name="attn_ctx",
    )(q_arr, k_arr, v_arr, out_prev)


def _attn_a_kernel(q_ref, kc_ref, kl_ref, vc_ref, vl_ref, o_ref, *, nq, scale):
    kc, kl, vc, vl = kc_ref[...], kl_ref[...], _with_ones(vc_ref[...]), _with_ones(vl_ref[...])
    for h in range(nq):
        qs = slice(h * HEAD_DIM, (h + 1) * HEAD_DIM)
        q = q_ref[:, qs]
        o = _softmax_pv([_dot_nt(q, kc), _dot_nt(q, kl)], [vc, vl], scale)
        o_ref[:, qs] = o.astype(o_ref.dtype)


def attn_a(rows, qa, ka, px, v0, tq=512):
    nq = A_HEADS // A_KV_HEADS
    qw = nq * HEAD_DIM
    n_q_blocks = rows.seq // tq
    lat0 = rows.nc // rows.seq
    return pl.pallas_call(
        functools.partial(_attn_a_kernel, nq=nq, scale=HEAD_DIM ** -0.5),
        grid=(rows.batch, A_KV_HEADS, n_q_blocks),
        in_specs=[pl.BlockSpec((tq, qw), lambda b, g, i: (rows.nc // tq + b * n_q_blocks + i, g)),
                  pl.BlockSpec((rows.ctx_len, HEAD_DIM), lambda b, g, i: (b, g)),
                  pl.BlockSpec((rows.seq, HEAD_DIM), lambda b, g, i: (lat0 + b, g)),
                  pl.BlockSpec((rows.ctx_len, HEAD_DIM), lambda b, g, i: (b, v0 // HEAD_DIM + g)),
                  pl.BlockSpec((rows.seq, HEAD_DIM), lambda b, g, i: (lat0 + b, v0 // HEAD_DIM + g))],
        out_specs=pl.BlockSpec((tq, qw), lambda b, g, i: (rows.nc // tq + b * n_q_blocks + i, g)),
        out_shape=jax.ShapeDtypeStruct((rows.n, A_HEADS * HEAD_DIM), BF16),
        compiler_params=_params("arbitrary", "arbitrary", "arbitrary"),
        name="attn_a",
    )(qa, ka, ka, px, px)


NB_QROWS = 8
NB_KROWS = 16


def _nb_slab_start(i, grid_rows):
    lo = i * NB_QROWS - NA_KH_MAX // 2
    hi = grid_rows - NB_KROWS
    if isinstance(i, int):
        return min(max(lo, 0), hi)
    return jnp.clip(lo, 0, hi)


def _nb_bias_kernel(rpb_ref, o_ref, *, grid_rows, inv_scale):
    h = pl.program_id(0)
    w = GRID_W
    rpb_h, rpb_w = 2 * NA_KH_MAX - 1, 2 * NA_KW - 1
    qc = lax.broadcasted_iota(jnp.int32, (w, w), 0)
    kc = lax.broadcasted_iota(jnp.int32, (w, w), 1)
    cs = jnp.clip(qc - NA_KW // 2, 0, w - NA_KW)
    col_ok = (kc >= cs) & (kc < cs + NA_KW)
    dc = kc - qc + NA_KW - 1
    dc_masks = [dc == j for j in range(rpb_w)]
    tiles = []
    for dr in range(rpb_h):
        t = jnp.zeros((w, w), F32)
        for j in range(rpb_w):
            t = jnp.where(dc_masks[j], rpb_ref[(h * rpb_h + dr) * rpb_w + j], t)
        tiles.append(jnp.where(col_ok, t * inv_scale, NEG_INF))
    blank = jnp.full((w, w), NEG_INF, F32)
    for i in range(grid_rows // NB_QROWS):
        k0 = _nb_slab_start(i, grid_rows)
        for qr in range(NB_QROWS):
            r = i * NB_QROWS + qr
            rs = min(max(r - NA_KH_MAX // 2, 0), grid_rows - NA_KH_MAX)
            row = []
            for kr in range(NB_KROWS):
                key_row = k0 + kr
                ok = rs <= key_row < rs + NA_KH_MAX
                row.append(tiles[key_row - r + NA_KH_MAX - 1] if ok else blank)
            o_ref[i, qr * w:(qr + 1) * w, :] = jnp.concatenate(row, axis=1)


def nb_bias(rpb, grid_rows):
    n_heads = rpb.shape[0]
    nqb = grid_rows // NB_QROWS
    tq, tk = NB_QROWS * GRID_W, NB_KROWS * GRID_W
    return pl.pallas_call(
        functools.partial(_nb_bias_kernel, grid_rows=grid_rows, inv_scale=HEAD_DIM ** 0.5),
        grid=(n_heads,),
        in_specs=[pl.BlockSpec(memory_space=pltpu.SMEM)],
        out_specs=pl.BlockSpec((None, nqb, tq, tk), lambda h: (h, 0, 0, 0)),
        out_shape=jax.ShapeDtypeStruct((n_heads, nqb, tq, tk), F32),
        compiler_params=_params("arbitrary"),
        name="nb_bias",
    )(rpb.reshape(-1))


def _attn_b_kernel(q_ref, kc_ref, kl_ref, vc_ref, vl_ref, bias_ref, o_ref, *, nh, scale, grid_rows):
    i = pl.program_id(1)
    tk = NB_KROWS * GRID_W
    start = pl.multiple_of(_nb_slab_start(i, grid_rows) * GRID_W, GRID_W)
    for h in range(nh):
        hs = slice(h * HEAD_DIM, (h + 1) * HEAD_DIM)
        q = q_ref[:, hs]
        s_c = _dot_nt(q, kc_ref[:, hs])
        s_w = _dot_nt(q, kl_ref[pl.ds(start, tk), hs]) + bias_ref[h]
        o = _softmax_pv([s_c, s_w], [_with_ones(vc_ref[:, hs]), _with_ones(vl_ref[pl.ds(start, tk), hs])], scale)
        o_ref[:, hs] = o.astype(o_ref.dtype)


def attn_b(rows, px, q0, k0, v0, bias, nh=2):
    grid_rows = rows.seq // GRID_W
    tq = NB_QROWS * GRID_W
    nqb = rows.seq // tq
    hw = nh * HEAD_DIM
    lat0 = rows.nc // rows.seq
    q_map = lambda g, i, b: (rows.nc // tq + b * nqb + i, q0 // hw + g)
    return pl.pallas_call(
        functools.partial(_attn_b_kernel, nh=nh, scale=HEAD_DIM ** -0.5, grid_rows=grid_rows),
        grid=(B_HEADS // nh, nqb, rows.batch),
        in_specs=[pl.BlockSpec((tq, hw), q_map),
                  pl.BlockSpec((rows.ctx_len, hw), lambda g, i, b: (b, k0 // hw + g)),
                  pl.BlockSpec((rows.seq, hw), lambda g, i, b: (lat0 + b, k0 // hw + g)),
                  pl.BlockSpec((rows.ctx_len, hw), lambda g, i, b: (b, v0 // hw + g)),
                  pl.BlockSpec((rows.seq, hw), lambda g, i, b: (lat0 + b, v0 // hw + g)),
                  pl.BlockSpec((nh, None, tq, NB_KROWS * GRID_W), lambda g, i, b: (g, i, 0, 0))],
        out_specs=pl.BlockSpec((tq, hw), lambda g, i, b: (rows.nc // tq + b * nqb + i, g)),
        out_shape=jax.ShapeDtypeStruct((rows.n, B_HEADS * HEAD_DIM), BF16),
        compiler_params=_params("arbitrary", "arbitrary", "arbitrary"),
        name="attn_b",
    )(px, px, px, px, px, bias)


def _merge_kernel(oa_ref, ob_ref, wa_ref, wb_ref, ga_ref, gb_ref, o_ref):
    a = _dot(oa_ref[...], wa_ref[...])
    b = _dot(ob_ref[...], wb_ref[...])
    y = _sigmoid(ga_ref[...].astype(F32)) * a + _sigmoid(gb_ref[...].astype(F32)) * b
    o_ref[...] = y.astype(o_ref.dtype)


def merge(oa, ob, wa, wb, l, px, ga0, gb0, row0, tm=1024, tn=1024):
    n, k = oa.shape
    d = wa.shape[2]
    t0 = row0 // tm
    return pl.pallas_call(
        _merge_kernel,
        grid=((n - row0) // tm, d // tn),
        in_specs=[pl.BlockSpec((tm, k), lambda i, j: (i + t0, 0)),
                  pl.BlockSpec((tm, k), lambda i, j: (i + t0, 0)),
                  pl.BlockSpec((None, k, tn), lambda i, j: (l, 0, j)),
                  pl.BlockSpec((None, k, tn), lambda i, j: (l, 0, j)),
                  pl.BlockSpec((tm, tn), lambda i, j: (i + t0, ga0 // tn + j)),
                  pl.BlockSpec((tm, tn), lambda i, j: (i + t0, gb0 // tn + j))],
        out_specs=pl.BlockSpec((tm, tn), lambda i, j: (i + t0, j)),
        out_shape=jax.ShapeDtypeStruct((n, d), BF16),
        compiler_params=_params("arbitrary", "arbitrary"),
        name="merge",
    )(oa, ob, wa, wb, px, px)


def _resid_mm_kernel(y_ref, w_ref, x_ref, g_ref, o_ref):
    o_ref[...] = x_ref[...] + g_ref[...] * _dot(y_ref[...], w_ref[...])


def resid_matmul(rows, y, w, l, x, gate, row0, tm, tn=1024):
    n, k = y.shape
    d = w.shape[2]
    t0 = row0 // tm
    return pl.pallas_call(
        _resid_mm_kernel,
        grid=((n - row0) // tm, d // tn),
        in_specs=[pl.BlockSpec((tm, k), lambda i, j: (i + t0, 0)),
                  pl.BlockSpec((None, k, tn), lambda i, j: (l, 0, j)),
                  pl.BlockSpec((tm, tn), lambda i, j: (i + t0, j)),
                  pl.BlockSpec((None, 1, tn), lambda i, j: (rows.mod_index((i + t0) * tm), 0, j))],
        out_specs=pl.BlockSpec((tm, tn), lambda i, j: (i + t0, j)),
        out_shape=jax.ShapeDtypeStruct(x.shape, x.dtype),
        input_output_aliases={2: 0},
        compiler_params=_params("arbitrary", "arbitrary"),
        name="resid_matmul",
    )(y, w, x, gate)


def _pack_words(v):
    half = v.shape[1] // 2
    bits = lax.bitcast_convert_type(v.astype(BF16).astype(F32), U32)
    return (bits[:, :half] >> 16) | bits[:, half:]


def _unpack_words(w):
    lo = lax.bitcast_convert_type(w << 16, F32)
    hi = lax.bitcast_convert_type(w & U32(0xFFFF0000), F32)
    return lo, hi


def _store_token_major(ref, words):
    rows, nch = words.shape[0], words.shape[1] // LANES
    for c in range(nch):
        ref[pl.ds(c, rows, stride=nch), :] = words[:, c * LANES:(c + 1) * LANES]


def _load_token_major(ref, rows, nch):
    return [ref[pl.ds(c, rows, stride=nch), :] for c in range(nch)]


def _route(logits, bias, n_experts):
    lane = lax.broadcasted_iota(jnp.int32, logits.shape, 1)
    lane_f = lane.astype(F32)
    per_group = n_experts // N_GROUPS
    valid = lane < n_experts
    scores = _sigmoid(logits)
    choice = jnp.where(valid, scores + bias, NEG_INF)
    big = float(LANES)

    def first_argmax(x):
        m = jnp.max(x, axis=-1, keepdims=True)
        idx = jnp.min(jnp.where(x == m, lane_f, big), axis=-1, keepdims=True)
        return m, idx

    in_group = [(lane >= g * per_group) & (lane < (g + 1) * per_group) for g in range(N_GROUPS)]
    group_score = []
    for g in range(N_GROUPS):
        xg = jnp.where(in_group[g], choice, NEG_INF)
        m1, i1 = first_argmax(xg)
        m2 = jnp.max(jnp.where(lane_f == i1, NEG_INF, xg), axis=-1, keepdims=True)
        group_score.append(m1 + m2)
    ok = jnp.zeros(logits.shape, F32)
    for g in range(N_GROUPS):
        ahead = jnp.zeros(group_score[g].shape, F32)
        for o in range(N_GROUPS):
            if o != g:
                wins = group_score[o] >= group_score[g] if o < g else group_score[o] > group_score[g]
                ahead = ahead + jnp.where(wins, 1.0, 0.0)
        ok = jnp.where(in_group[g], jnp.where(ahead < TOPK_GROUPS, 1.0, 0.0), ok)
    x = jnp.where(ok > 0.5, choice, NEG_INF)
    idxs, ws = [], []
    for _ in range(TOP_K):
        _, idx = first_argmax(x)
        hit = lane_f == idx
        idxs.append(idx)
        ws.append(jnp.sum(jnp.where(hit, scores, 0.0), axis=-1, keepdims=True))
        x = jnp.where(hit, NEG_INF, x)
    total = functools.reduce(jnp.add, ws)
    idx_tile = jnp.zeros(logits.shape, F32)
    w_tile = jnp.zeros(logits.shape, F32)
    for j in range(TOP_K):
        idx_tile = jnp.where(lane == j, idxs[j], idx_tile)
        w_tile = jnp.where(lane == j, ws[j] / total * ROUTED_SCALE, w_tile)
    return idx_tile.astype(jnp.int32), w_tile


def _norm_router_kernel(x_ref, g_ref, sc_ref, sh_ref, wh_ref, wl_ref, br_ref,
                        h_ref, htok_ref, idx_ref, wgt_ref, *, n_experts):
    y = _rmsnorm_rows(x_ref[...], g_ref[...])
    h = y * (1.0 + sc_ref[...]) + sh_ref[...]
    h_hi = h.astype(BF16)
    h_ref[...] = h_hi
    _store_token_major(htok_ref, _pack_words(h))
    h_lo = (h - h_hi.astype(F32)).astype(BF16)
    wh = wh_ref[...]
    logits = _dot(h_hi, wh) + _dot(h_lo, wh) + _dot(h_hi, wl_ref[...])
    idx_ref[...], wgt_ref[...] = _route(logits, br_ref[...], n_experts)


def norm_router(rows, x, g, sc, sh, w_router, b_router, row0, tm=256):
    n, d = x.shape
    e = w_router.shape[1]
    tok_rows = d // 2 // LANES
    t0 = row0 // tm
    wr = jnp.zeros((d, LANES), F32).at[:, :e].set(w_router)
    wr_hi = wr.astype(BF16)
    wr_lo = (wr - wr_hi.astype(F32)).astype(BF16)
    br = jnp.zeros((1, LANES), F32).at[0, :e].set(b_router)
    mod_map = lambda i: (rows.mod_index((i + t0) * tm), 0, 0)
    const = lambda i: (0, 0)
    row_map = lambda i: (i + t0, 0)
    return pl.pallas_call(
        functools.partial(_norm_router_kernel, n_experts=e),
        grid=((n - row0) // tm,),
        in_specs=[pl.BlockSpec((tm, d), row_map),
                  pl.BlockSpec((1, d), const),
                  pl.BlockSpec((None, 1, d), mod_map),
                  pl.BlockSpec((None, 1, d), mod_map),
                  pl.BlockSpec((d, LANES), const),
                  pl.BlockSpec((d, LANES), const),
                  pl.BlockSpec((1, LANES), const)],
        out_specs=[pl.BlockSpec((tm, d), row_map),
                   pl.BlockSpec((tm * tok_rows, LANES), row_map),
                   pl.BlockSpec((tm, LANES), row_map),
                   pl.BlockSpec((tm, LANES), row_map)],
        out_shape=[jax.ShapeDtypeStruct((n, d), BF16),
                   jax.ShapeDtypeStruct((n * tok_rows, LANES), U32),
                   jax.ShapeDtypeStruct((n, LANES), jnp.int32),
                   jax.ShapeDtypeStruct((n, LANES), F32)],
        compiler_params=_params("arbitrary"),
        name="norm_router",
    )(x, g.reshape(1, d), sc, sh, wr_hi, wr_lo, br)


def moe_plan(idx, wgt, tok0, n_experts, n_rows, tok_rows):
    t, p = idx.shape[0], idx.shape[0] * TOP_K
    n_tiles = p // MOE_TM + n_experts
    assert n_tiles % 2 == 0 and p % MOE_TM == 0 and 2 * MOE_TM <= n_rows
    flat_e = idx.reshape(p)
    order = jnp.argsort(flat_e, stable=True).astype(jnp.int32)
    counts = jnp.sum((flat_e[:, None] == jnp.arange(n_experts, dtype=jnp.int32)[None, :]).astype(jnp.int32), axis=0)
    tiles_per = (counts + MOE_TM - 1) // MOE_TM
    tile_end = jnp.cumsum(tiles_per)
    tile_start = tile_end - tiles_per
    pair_start = jnp.cumsum(counts) - counts
    tile = jnp.arange(n_tiles, dtype=jnp.int32)
    te = jnp.sum((tile[:, None] >= tile_end[None, :]).astype(jnp.int32), axis=1)
    te = jnp.minimum(te, n_experts - 1)
    lane = jnp.arange(MOE_TM, dtype=jnp.int32)[None, :]
    r = (tile - tile_start[te])[:, None] * MOE_TM + lane
    valid = (r < counts[te][:, None]) & (tile < tile_end[-1])[:, None]
    pair = order[jnp.clip(pair_start[te][:, None] + r, 0, p - 1)]
    tok = pair // TOP_K + tok0
    plane = pair % TOP_K
    slot = tile[:, None] * MOE_TM + lane
    src = jnp.where(valid, tok, tok0 + slot % t)
    dump = TOP_K * n_rows + (tile % 2)[:, None] * MOE_TM + lane
    dst = jnp.where(valid, plane * n_rows + tok, dump)
    wrow = jnp.where(valid, wgt.reshape(p)[pair], 0.0).reshape(n_tiles * MOE_TM, 1)
    src, dst = src * tok_rows, dst * tok_rows
    pad = jnp.zeros((n_tiles // 2, IDX_ROWS - 2, MOE_TM), jnp.int32)
    blk = jnp.concatenate([jnp.stack([src[0::2], src[1::2]], axis=1), pad,
                           jnp.stack([dst[0::2], dst[1::2]], axis=1), pad], axis=1).astype(jnp.int32)
    used_steps = ((tile_end[-1] + 1) // 2).astype(jnp.int32).reshape(1)
    return te, used_steps, blk, wrow


IDX_ROWS = 8


def _stage_pitch(tok_rows):
    return tok_rows if (tok_rows // 8) % 2 == 1 else tok_rows + 8


def _moe_expert_kernel(te_ref, ns_ref, idx_hbm, htok_hbm, wrow_ref,
                       wga_ref, wua_ref, wda_ref, wgb_ref, wub_ref, wdb_ref,
                       yt_hbm,
                       src_s, dst_s, in_a, in_b, out_a, out_b, idx_sem, in_sem, out_sem,
                       *, n_rows, tok_rows):
    del te_ref
    n_steps = ns_ref[0]
    k = pl.program_id(0)
    tm, nch = MOE_TM, tok_rows
    pitch = _stage_pitch(nch)
    stage_in, stage_out = (in_a, in_b), (out_a, out_b)
    dump0 = TOP_K * n_rows * nch

    def src_copy(step):
        return pltpu.make_async_copy(idx_hbm.at[step, pl.ds(0, IDX_ROWS), :], src_s, idx_sem.at[0])

    def dst_copy(step):
        return pltpu.make_async_copy(idx_hbm.at[step, pl.ds(IDX_ROWS, IDX_ROWS), :], dst_s, idx_sem.at[1])

    def issue_gathers(tile):
        for r in range(tm):
            off = pl.multiple_of(src_s[tile, r], nch)
            pltpu.make_async_copy(htok_hbm.at[pl.ds(off, nch), :],
                                  stage_in[tile].at[pl.ds(r * pitch, nch), :], in_sem.at[tile]).start()

    def wait_gathers(tile):
        pltpu.make_async_copy(htok_hbm.at[pl.ds(0, tm * nch), :], stage_in[tile].at[pl.ds(0, tm * nch), :],
                              in_sem.at[tile]).wait()

    def issue_scatters(tile):
        for r in range(tm):
            off = pl.multiple_of(dst_s[tile, r], nch)
            pltpu.make_async_copy(stage_out[tile].at[pl.ds(r * pitch, nch), :],
                                  yt_hbm.at[pl.ds(off, nch), :], out_sem.at[tile]).start()

    def bulk_scatter(tile):
        return pltpu.make_async_copy(stage_out[tile].at[pl.ds(0, tm * nch), :],
                                     yt_hbm.at[pl.ds(dump0 + tile * tm * nch, tm * nch), :], out_sem.at[tile])

    def load_rows(tile):
        los, his = [], []
        for c in range(nch):
            lo, hi = _unpack_words(stage_in[tile][pl.ds(c, tm, stride=pitch), :])
            los.append(lo.astype(BF16))
            his.append(hi.astype(BF16))
        return jnp.concatenate(los + his, axis=1)

    def compute(tile, wg_ref, wu_ref, wd_ref):
        x = load_rows(tile)
        hdim = wg_ref.shape[1]
        wgu = jnp.concatenate([wg_ref[...].astype(BF16), wu_ref[...].astype(BF16)], axis=1)
        gu = _dot(x, wgu)
        g, u = gu[:, :hdim], gu[:, hdim:]
        hid = (g * _sigmoid(g)) * u * wrow_ref[tile * tm:(tile + 1) * tm, :]
        y = _dot(hid.astype(BF16), wd_ref[...].astype(BF16))
        words = _pack_words(y)
        for c in range(nch):
            stage_out[tile][pl.ds(c, tm, stride=pitch), :] = words[:, c * LANES:(c + 1) * LANES]

    @pl.when(k < n_steps)
    def _():
        last = k == n_steps - 1
        next_step = jnp.where(last, 0, k + 1)

        @pl.when(k == 0)
        def _():
            first = src_copy(0)
            first.start()
            first.wait()
            issue_gathers(0)
            issue_gathers(1)
            src_copy(next_step).start()
            dst_copy(0).start()
            for tile in range(2):
                stage_out[tile][...] = jnp.zeros(stage_out[tile].shape, U32)
                bulk_scatter(tile).start()

        wait_gathers(0)
        bulk_scatter(0).wait()
        compute(0, wga_ref, wua_ref, wda_ref)
        dst_copy(k).wait()
        issue_scatters(0)
        src_copy(next_step).wait()
        issue_gathers(0)

        wait_gathers(1)
        bulk_scatter(1).wait()
        compute(1, wgb_ref, wub_ref, wdb_ref)
        issue_scatters(1)
        issue_gathers(1)

        @pl.when(jnp.logical_not(last))
        def _():
            src_copy(jnp.where(k + 2 < n_steps, k + 2, 0)).start()
            dst_copy(k + 1).start()

        @pl.when(last)
        def _():
            for tile in range(2):
                wait_gathers(tile)
                bulk_scatter(tile).wait()


def moe_experts(htok, te, used_steps, blk, wrow, wg, wu, wd, l, n_rows):
    n_steps = blk.shape[0]
    tok_rows = htok.shape[0] // n_rows
    _, _, d, hdim = wg.shape
    stage = pltpu.VMEM((MOE_TM * _stage_pitch(tok_rows), LANES), U32)
    w_in_spec = lambda off: pl.BlockSpec((None, None, d, hdim), lambda k, te, ns: (l, te[2 * k + off], 0, 0))
    w_out_spec = lambda off: pl.BlockSpec((None, None, hdim, d), lambda k, te, ns: (l, te[2 * k + off], 0, 0))
    return pl.pallas_call(
        functools.partial(_moe_expert_kernel, n_rows=n_rows, tok_rows=tok_rows),
        grid_spec=pltpu.PrefetchScalarGridSpec(
            num_scalar_prefetch=2,
            grid=(n_steps,),
            in_specs=[pl.BlockSpec(memory_space=pl.ANY),
                      pl.BlockSpec(memory_space=pl.ANY),
                      pl.BlockSpec((2 * MOE_TM, 1), lambda k, te, ns: (k, 0)),
                      w_in_spec(0), w_in_spec(0), w_out_spec(0),
                      w_in_spec(1), w_in_spec(1), w_out_spec(1)],
            out_specs=pl.BlockSpec(memory_space=pl.ANY),
            scratch_shapes=[pltpu.SMEM((IDX_ROWS, MOE_TM), jnp.int32),
                            pltpu.SMEM((IDX_ROWS, MOE_TM), jnp.int32),
                            stage, stage, stage, stage,
                            pltpu.SemaphoreType.DMA((2,)),
                            pltpu.SemaphoreType.DMA((2,)),
                            pltpu.SemaphoreType.DMA((2,))],
        ),
        out_shape=jax.ShapeDtypeStruct(((TOP_K + 1) * n_rows * tok_rows, LANES), U32),
        compiler_params=_params("arbitrary"),
        name="moe_experts",
    )(te, used_steps, blk, htok, wrow, wg, wu, wd, wg, wu, wd)


def _moe_combine_kernel(yt_ref, h_ref, wsg_ref, wsu_ref, wsd_ref, x_ref, g_ref, ng_ref, nsc_ref, nsh_ref,
                        o_ref, n_ref, acc_lo, acc_hi, *, tok_rows):
    tm = x_ref.shape[0]
    lo, hi = _unpack_words(yt_ref[0])
    for j in range(1, TOP_K):
        l2, h2 = _unpack_words(yt_ref[j])
        lo, hi = lo + l2, hi + h2
    acc_lo[...] = lo
    acc_hi[...] = hi
    routed = jnp.concatenate(_load_token_major(acc_lo, tm, tok_rows) + _load_token_major(acc_hi, tm, tok_rows), axis=1)
    h = h_ref[...]
    g = _dot(h, wsg_ref[...])
    u = _dot(h, wsu_ref[...])
    shared = _dot(((g * _sigmoid(g)) * u).astype(BF16), wsd_ref[...])
    o = x_ref[...] + g_ref[...] * (routed + shared)
    o_ref[...] = o
    y = _rmsnorm_rows(o, ng_ref[...])
    n_ref[...] = (y * (1.0 + nsc_ref[...]) + nsh_ref[...]).astype(n_ref.dtype)


def moe_combine(rows, yt, h, wsg, wsu, wsd, l, x, gate, next_g, next_sc, next_sh, next_dtype, row0, tm=128):
    n, d = x.shape
    tok_rows = d // 2 // LANES
    yt = yt.reshape(TOP_K + 1, n * tok_rows, LANES)
    s = wsg.shape[2]
    t0 = row0 // tm
    row_map = lambda i: (i + t0, 0)
    mod_map = lambda i: (rows.mod_index((i + t0) * tm), 0, 0)
    return pl.pallas_call(
        functools.partial(_moe_combine_kernel, tok_rows=tok_rows),
        grid=((n - row0) // tm,),
        in_specs=[pl.BlockSpec((TOP_K, tm * tok_rows, LANES), lambda i: (0, i + t0, 0)),
                  pl.BlockSpec((tm, d), row_map),
                  pl.BlockSpec((None, d, s), lambda i: (l, 0, 0)),
                  pl.BlockSpec((None, d, s), lambda i: (l, 0, 0)),
                  pl.BlockSpec((None, s, d), lambda i: (l, 0, 0)),
                  pl.BlockSpec((tm, d), row_map),
                  pl.BlockSpec((None, 1, d), mod_map),
                  pl.BlockSpec((1, d), lambda i: (0, 0)),
                  pl.BlockSpec((None, 1, d), mod_map),
                  pl.BlockSpec((None, 1, d), mod_map)],
        out_specs=[pl.BlockSpec((tm, d), row_map),
                   pl.BlockSpec((tm, d), lambda i: (i, 0))],
        out_shape=[jax.ShapeDtypeStruct(x.shape, x.dtype),
                   jax.ShapeDtypeStruct((n - row0, d), next_dtype)],
        scratch_shapes=[pltpu.VMEM((tm * tok_rows, LANES), F32), pltpu.VMEM((tm * tok_rows, LANES), F32)],
        input_output_aliases={5: 0},
        compiler_params=_params("arbitrary"),
        name="moe_combine",
    )(yt, h, wsg, wsu, wsd, x, gate, next_g.reshape(1, d), next_sc, next_sh)


def kernel(x, c, ctx, c_ctx, g_attn, g_ffn, w_mod_down, w_mod_up, b_mod_up, w_in, q_norm, k_norm, rpb, w_branch_a, w_branch_b, w_out, w_router, b_router, w_exp_gate, w_exp_up, w_exp_down, w_sh_gate, w_sh_up, w_sh_down, g_final):
    batch, seq, d = x.shape
    ctx_len = ctx.shape[1]
    depth = w_in.shape[0]
    n_experts = w_exp_gate.shape[1]
    rows = Rows(batch, ctx_len, seq)
    assert rows.nc % rows.seq == 0 and batch < MOD_ROWS

    a_q, a_kv, b_w = A_HEADS * HEAD_DIM, A_KV_HEADS * HEAD_DIM, B_HEADS * HEAD_DIM
    col_qa = 0
    col_qb = col_qa + a_q
    col_ka = col_qb + b_w
    col_va = col_ka + a_kv
    col_kb = col_va + a_kv
    col_vb = col_kb + b_w
    col_ga = col_vb + b_w
    col_gb = col_ga + d

    cvec = jnp.zeros((MOD_ROWS, d), F32).at[:batch].set(c).at[batch].set(c_ctx)
    mod = modulation_all(cvec, w_mod_down, w_mod_up, b_mod_up)

    w_in_b, w_a_b, w_b_b, w_out_b = (w.astype(BF16) for w in (w_in, w_branch_a, w_branch_b, w_out))
    wsg_b, wsu_b, wsd_b = (w.astype(BF16) for w in (w_sh_gate, w_sh_up, w_sh_down))

    prep_tm = 256
    cos, sin = rope_tables(rows, prep_tm)
    xs = jnp.concatenate([ctx.reshape(rows.nc, d), x.reshape(rows.nx, d)], axis=0)

    mods = [[mod[l, :batch + 1, k * d:(k + 1) * d][:, None, :] for k in range(N_MOD)] for l in range(depth)]
    no_mod = jnp.zeros((batch + 1, 1, d), F32)
    h1 = norm_mod(rows, xs, g_attn[0], mods[0][1], mods[0][0], 0)
    for l in range(depth):
        update_ctx = l < depth - 1
        row0 = 0 if update_ctx else rows.nc
        sh1, sc1, gt1, sh2, sc2, gt2 = mods[l]

        px = matmul(h1, w_in_b, l)
        qa, ka = qk_prep(rows, px, (col_qa, a_q), (col_ka, a_kv), q_norm[l], k_norm[l], cos, sin, prep_tm)
        oa = attn_a(rows, qa, ka, px, col_va)
        ob = attn_b(rows, px, col_qb, col_kb, col_vb, nb_bias(rpb[l], seq // GRID_W))
        if update_ctx:
            oa = attn_ctx(rows, qa, 0, ka, 0, px, col_va, oa, A_HEADS, A_KV_HEADS)
            ob = attn_ctx(rows, px, col_qb, px, col_kb, px, col_vb, ob, B_HEADS, B_HEADS)
        y = merge(oa, ob, w_a_b, w_b_b, l, px, col_ga, col_gb, row0)
        xs = resid_matmul(rows, y, w_out_b, l, xs, gt1, row0, tm=1024)

        h2, htok, idx_t, wgt_t = norm_router(rows, xs, g_ffn[l], sc2, sh2, w_router[l], b_router[l], row0)
        te, used_steps, blk, wrow = moe_plan(idx_t[row0:, :TOP_K], wgt_t[row0:, :TOP_K], row0, n_experts, rows.n, d // 2 // LANES)
        yt = moe_experts(htok, te, used_steps, blk, wrow, w_exp_gate, w_exp_up, w_exp_down, l, rows.n)
        if update_ctx:
            nxt = (g_attn[l + 1], mods[l + 1][1], mods[l + 1][0], BF16)
        else:
            nxt = (g_final, no_mod, no_mod, F32)
        xs, h1 = moe_combine(rows, yt, h2, wsg_b, wsu_b, wsd_b, l, xs, gt2, *nxt, row0)

    return h1.reshape(batch, seq, d)
```

```python
import functools
import math

import jax
import jax.numpy as jnp
from jax import lax
from jax.experimental import pallas as pl
from jax.experimental.pallas import tpu as pltpu

GRID_W = 64
HEAD_DIM = 128
A_HEADS = 16
A_KV_HEADS = 4
B_HEADS = 16
NA_KH_MAX = 8
NA_KW = 16
ROPE_THETA = 10000.0
N_MOD = 6
TOP_K = 8
N_GROUPS = 8
TOPK_GROUPS = 4
ROUTED_SCALE = 2.5
EPS = 1e-6

F32 = jnp.float32
BF16 = jnp.bfloat16
U32 = jnp.uint32
HIGHEST = lax.Precision.HIGHEST
NEG_INF = float("-inf")
LANES = 128

VMEM_LIMIT_BYTES = 56 * 2**20
MOD_ROWS = 16
MOE_TM = 256


def _params(*sem):
    return pltpu.CompilerParams(dimension_semantics=sem, vmem_limit_bytes=VMEM_LIMIT_BYTES)


def _dot(a, b):
    return jnp.dot(a, b, preferred_element_type=F32)


def _dot_nt(a, b):
    return lax.dot_general(a, b, (((1,), (1,)), ((), ())), preferred_element_type=F32)


def _sigmoid(x):
    return 1.0 / (1.0 + jnp.exp(-x))


def _mod_down_kernel(c_ref, w_ref, o_ref):
    c = c_ref[...]
    o_ref[...] = jnp.dot(c * _sigmoid(c), w_ref[...], preferred_element_type=F32, precision=HIGHEST)


def _mod_up_kernel(t_ref, w_ref, b_ref, o_ref):
    o_ref[...] = jnp.dot(t_ref[...], w_ref[...], preferred_element_type=F32, precision=HIGHEST) + b_ref[...]


def modulation_all(cvec, w_down, w_up, b_up, tn=2048):
    depth, d, r = w_down.shape
    n = w_up.shape[2]
    t = pl.pallas_call(
        _mod_down_kernel,
        grid=(depth,),
        in_specs=[pl.BlockSpec((MOD_ROWS, d), lambda l: (0, 0)),
                  pl.BlockSpec((None, d, r), lambda l: (l, 0, 0))],
        out_specs=pl.BlockSpec((None, MOD_ROWS, r), lambda l: (l, 0, 0)),
        out_shape=jax.ShapeDtypeStruct((depth, MOD_ROWS, r), F32),
        compiler_params=_params("arbitrary"),
        name="mod_down",
    )(cvec, w_down)
    return pl.pallas_call(
        _mod_up_kernel,
        grid=(depth, n // tn),
        in_specs=[pl.BlockSpec((None, MOD_ROWS, r), lambda l, j: (l, 0, 0)),
                  pl.BlockSpec((None, r, tn), lambda l, j: (l, 0, j)),
                  pl.BlockSpec((None, 1, tn), lambda l, j: (l, 0, j))],
        out_specs=pl.BlockSpec((None, MOD_ROWS, tn), lambda l, j: (l, 0, j)),
        out_shape=jax.ShapeDtypeStruct((depth, MOD_ROWS, n), F32),
        compiler_params=_params("arbitrary", "arbitrary"),
        name="mod_up",
    )(t, w_up, b_up.reshape(depth, 1, n))


class Rows:
    def __init__(self, batch, ctx_len, seq):
        self.batch, self.ctx_len, self.seq = batch, ctx_len, seq
        self.nc = batch * ctx_len
        self.nx = batch * seq
        self.n = self.nc + self.nx

    def mod_index(self, row):
        return jnp.where(row < self.nc, self.batch, (row - self.nc) // self.seq)


def _rmsnorm_rows(x, g):
    ms = jnp.mean(x * x, axis=-1, keepdims=True)
    return x * lax.rsqrt(ms + EPS) * g


def _norm_mod_kernel(x_ref, g_ref, sc_ref, sh_ref, o_ref):
    y = _rmsnorm_rows(x_ref[...], g_ref[...])
    o_ref[...] = (y * (1.0 + sc_ref[...]) + sh_ref[...]).astype(o_ref.dtype)


def norm_mod(rows, x, g, sc, sh, row0, tm=256):
    n, d = x.shape
    t0 = row0 // tm
    mod_map = lambda i: (rows.mod_index((i + t0) * tm), 0, 0)
    return pl.pallas_call(
        _norm_mod_kernel,
        grid=((n - row0) // tm,),
        in_specs=[pl.BlockSpec((tm, d), lambda i: (i + t0, 0)),
                  pl.BlockSpec((1, d), lambda i: (0, 0)),
                  pl.BlockSpec((None, 1, d), mod_map),
                  pl.BlockSpec((None, 1, d), mod_map)],
        out_specs=pl.BlockSpec((tm, d), lambda i: (i + t0, 0)),
        out_shape=jax.ShapeDtypeStruct((n, d), BF16),
        compiler_params=_params("arbitrary"),
        name="norm_mod",
    )(x, g.reshape(1, d), sc, sh)


def _mm_kernel(x_ref, w_ref, o_ref):
    o_ref[...] = _dot(x_ref[...], w_ref[...]).astype(o_ref.dtype)


def matmul(x, w, l, tm=1024, tn=1024):
    m, k = x.shape
    n = w.shape[2]
    return pl.pallas_call(
        _mm_kernel,
        grid=(m // tm, n // tn),
        in_specs=[pl.BlockSpec((tm, k), lambda i, j: (i, 0)),
                  pl.BlockSpec((None, k, tn), lambda i, j: (l, 0, j))],
        out_specs=pl.BlockSpec((tm, tn), lambda i, j: (i, j)),
        out_shape=jax.ShapeDtypeStruct((m, n), BF16),
        compiler_params=_params("arbitrary", "arbitrary"),
        name="matmul",
    )(x, w)


def rope_tables(rows, tm):
    half = HEAD_DIM // 2
    quarter = half // 2
    pos = jnp.arange(rows.seq)
    inv_freq = ROPE_THETA ** (-jnp.arange(quarter, dtype=F32) / quarter)
    ang_r = (pos // GRID_W).astype(F32)[:, None] * inv_freq[None, :]
    ang_c = (pos % GRID_W).astype(F32)[:, None] * inv_freq[None, :]
    ang = jnp.concatenate([ang_r, ang_r, ang_c, ang_c], axis=-1)
    sign = jnp.concatenate([-jnp.ones(quarter), jnp.ones(quarter)] * 2).astype(F32)
    cos = jnp.concatenate([jnp.ones((tm, HEAD_DIM), F32), jnp.cos(ang)], axis=0)
    sin = jnp.concatenate([jnp.zeros((tm, HEAD_DIM), F32), jnp.sin(ang) * sign[None, :]], axis=0)
    return cos, sin


def _qk_prep_kernel(q_ref, k_ref, cos_ref, sin_ref, qg_ref, kg_ref, qo_ref, ko_ref):
    cos = cos_ref[...]
    sin = sin_ref[...]
    quarter = HEAD_DIM // 4
    lane = lax.broadcasted_iota(jnp.int32, cos.shape, 1)
    first = (lane & quarter) == 0

    def prep(x_ref, g, o_ref):
        for h in range(x_ref.shape[1] // HEAD_DIM):
            sl = slice(h * HEAD_DIM, (h + 1) * HEAD_DIM)
            y = _rmsnorm_rows(x_ref[:, sl].astype(F32), g)
            partner = jnp.where(first, pltpu.roll(y, HEAD_DIM - quarter, 1), pltpu.roll(y, quarter, 1))
            o_ref[:, sl] = (y * cos + partner * sin).astype(o_ref.dtype)

    prep(q_ref, qg_ref[...], qo_ref)
    prep(k_ref, kg_ref[...], ko_ref)


def qk_prep(rows, px, q_cols, k_cols, q_norm, k_norm, cos, sin, tm=256):
    n = px.shape[0]
    (q0, qw), (k0, kw) = q_cols, k_cols
    per_seq = rows.seq // tm

    def tab_map(i):
        r = i * tm
        return (jnp.where(r < rows.nc, 0, 1 + ((r - rows.nc) // tm) % per_seq), 0)

    return pl.pallas_call(
        _qk_prep_kernel,
        grid=(n // tm,),
        in_specs=[pl.BlockSpec((tm, qw), lambda i: (i, q0 // qw)),
                  pl.BlockSpec((tm, kw), lambda i: (i, k0 // kw)),
                  pl.BlockSpec((tm, HEAD_DIM), tab_map),
                  pl.BlockSpec((tm, HEAD_DIM), tab_map),
                  pl.BlockSpec((1, HEAD_DIM), lambda i: (0, 0)),
                  pl.BlockSpec((1, HEAD_DIM), lambda i: (0, 0))],
        out_specs=[pl.BlockSpec((tm, qw), lambda i: (i, 0)),
                   pl.BlockSpec((tm, kw), lambda i: (i, 0))],
        out_shape=[jax.ShapeDtypeStruct((n, qw), BF16), jax.ShapeDtypeStruct((n, kw), BF16)],
        compiler_params=_params("arbitrary"),
        name="qk_prep",
    )(px, px, cos, sin, q_norm.reshape(1, HEAD_DIM), k_norm.reshape(1, HEAD_DIM))


def _with_ones(v):
    return jnp.concatenate([v, jnp.ones_like(v)], axis=1)


def _softmax_pv(scores, values, scale):
    c = scale * math.log2(math.e)
    d = values[0].shape[1] // 2
    m = functools.reduce(jnp.maximum, [jnp.max(s, axis=-1, keepdims=True) for s in scores])
    o = functools.reduce(jnp.add, [_dot(jnp.exp2((s - m) * c).astype(v.dtype), v) for s, v in zip(scores, values)])
    return o[:, :d] / o[:, d:d + 1]


def _attn_ctx_kernel(q_ref, k_ref, v_ref, prev_ref, o_ref, *, nq, nk, scale):
    del prev_ref
    for h in range(nq):
        kh = h * nk // nq
        ks = slice(kh * HEAD_DIM, (kh + 1) * HEAD_DIM)
        qs = slice(h * HEAD_DIM, (h + 1) * HEAD_DIM)
        s = _dot_nt(q_ref[:, qs], k_ref[:, ks])
        o_ref[:, qs] = _softmax_pv([s], [_with_ones(v_ref[:, ks])], scale).astype(o_ref.dtype)


def attn_ctx(rows, q_arr, q0, k_arr, k0, v_arr, v0, out_prev, n_heads, n_kv_heads, nq=4):
    nk = nq * n_kv_heads // n_heads
    qw, kw = nq * HEAD_DIM, nk * HEAD_DIM
    cl = rows.ctx_len
    return pl.pallas_call(
        functools.partial(_attn_ctx_kernel, nq=nq, nk=nk, scale=HEAD_DIM ** -0.5),
        grid=(rows.batch, n_heads // nq),
        in_specs=[pl.BlockSpec((cl, qw), lambda b, g: (b, q0 // qw + g)),
                  pl.BlockSpec((cl, kw), lambda b, g: (b, k0 // kw + g)),
                  pl.BlockSpec((cl, kw), lambda b, g: (b, v0 // kw + g)),
                  pl.BlockSpec(memory_space=pl.ANY)],
        out_specs=pl.BlockSpec((cl, qw), lambda b, g: (b, g)),
        out_shape=jax.ShapeDtypeStruct(out_prev.shape, out_prev.dtype),
        input_output_aliases={3: 0},
        compiler_params=_params("arbitrary", "arbitrary"),
        name="attn_ctx",
    )(q_arr, k_arr, v_arr, out_prev)


def _attn_a_kernel(q_ref, kc_ref, kl_ref, vc_ref, vl_ref, o_ref, *, nq, scale):
    kc, kl, vc, vl = kc_ref[...], kl_ref[...], _with_ones(vc_ref[...]), _with_ones(vl_ref[...])
    for h in range(nq):
        qs = slice(h * HEAD_DIM, (h + 1) * HEAD_DIM)
        q = q_ref[:, qs]
        o = _softmax_pv([_dot_nt(q, kc), _dot_nt(q, kl)], [vc, vl], scale)
        o_ref[:, qs] = o.astype(o_ref.dtype)


def attn_a(rows, qa, ka, px, v0, tq=512):
    nq = A_HEADS // A_KV_HEADS
    qw = nq * HEAD_DIM
    n_q_blocks = rows.seq // tq
    lat0 = rows.nc // rows.seq
    return pl.pallas_call(
        functools.partial(_attn_a_kernel, nq=nq, scale=HEAD_DIM ** -0.5),
        grid=(rows.batch, A_KV_HEADS, n_q_blocks),
        in_specs=[pl.BlockSpec((tq, qw), lambda b, g, i: (rows.nc // tq + b * n_q_blocks + i, g)),
                  pl.BlockSpec((rows.ctx_len, HEAD_DIM), lambda b, g, i: (b, g)),
                  pl.BlockSpec((rows.seq, HEAD_DIM), lambda b, g, i: (lat0 + b, g)),
                  pl.BlockSpec((rows.ctx_len, HEAD_DIM), lambda b, g, i: (b, v0 // HEAD_DIM + g)),
                  pl.BlockSpec((rows.seq, HEAD_DIM), lambda b, g, i: (lat0 + b, v0 // HEAD_DIM + g))],
        out_specs=pl.BlockSpec((tq, qw), lambda b, g, i: (rows.nc // tq + b * n_q_blocks + i, g)),
        out_shape=jax.ShapeDtypeStruct((rows.n, A_HEADS * HEAD_DIM), BF16),
        compiler_params=_params("arbitrary", "arbitrary", "arbitrary"),
        name="attn_a",
    )(qa, ka, ka, px, px)


NB_QROWS = 8
NB_KROWS = 16


def _nb_slab_start(i, grid_rows):
    lo = i * NB_QROWS - NA_KH_MAX // 2
    hi = grid_rows - NB_KROWS
    if isinstance(i, int):
        return min(max(lo, 0), hi)
    return jnp.clip(lo, 0, hi)


def _nb_bias_kernel(rpb_ref, o_ref, *, grid_rows, inv_scale):
    h = pl.program_id(0)
    w = GRID_W
    rpb_h, rpb_w = 2 * NA_KH_MAX - 1, 2 * NA_KW - 1
    qc = lax.broadcasted_iota(jnp.int32, (w, w), 0)
    kc = lax.broadcasted_iota(jnp.int32, (w, w), 1)
    cs = jnp.clip(qc - NA_KW // 2, 0, w - NA_KW)
    col_ok = (kc >= cs) & (kc < cs + NA_KW)
    dc = kc - qc + NA_KW - 1
    dc_masks = [dc == j for j in range(rpb_w)]
    tiles = []
    for dr in range(rpb_h):
        t = jnp.zeros((w, w), F32)
        for j in range(rpb_w):
            t = jnp.where(dc_masks[j], rpb_ref[(h * rpb_h + dr) * rpb_w + j], t)
        tiles.append(jnp.where(col_ok, t * inv_scale, NEG_INF))
    blank = jnp.full((w, w), NEG_INF, F32)
    for i in range(grid_rows // NB_QROWS):
        k0 = _nb_slab_start(i, grid_rows)
        for qr in range(NB_QROWS):
            r = i * NB_QROWS + qr
            rs = min(max(r - NA_KH_MAX // 2, 0), grid_rows - NA_KH_MAX)
            row = []
            for kr in range(NB_KROWS):
                key_row = k0 + kr
                ok = rs <= key_row < rs + NA_KH_MAX
                row.append(tiles[key_row - r + NA_KH_MAX - 1] if ok else blank)
            o_ref[i, qr * w:(qr + 1) * w, :] = jnp.concatenate(row, axis=1)


def nb_bias(rpb, grid_rows):
    n_heads = rpb.shape[0]
    nqb = grid_rows // NB_QROWS
    tq, tk = NB_QROWS * GRID_W, NB_KROWS * GRID_W
    return pl.pallas_call(
        functools.partial(_nb_bias_kernel, grid_rows=grid_rows, inv_scale=HEAD_DIM ** 0.5),
        grid=(n_heads,),
        in_specs=[pl.BlockSpec(memory_space=pltpu.SMEM)],
        out_specs=pl.BlockSpec((None, nqb, tq, tk), lambda h: (h, 0, 0, 0)),
        out_shape=jax.ShapeDtypeStruct((n_heads, nqb, tq, tk), F32),
        compiler_params=_params("arbitrary"),
        name="nb_bias",
    )(rpb.reshape(-1))


def _attn_b_kernel(q_ref, kc_ref, kl_ref, vc_ref, vl_ref, bias_ref, o_ref, *, nh, scale, grid_rows):
    i = pl.program_id(1)
    tk = NB_KROWS * GRID_W
    start = pl.multiple_of(_nb_slab_start(i, grid_rows) * GRID_W, GRID_W)
    for h in range(nh):
        hs = slice(h * HEAD_DIM, (h + 1) * HEAD_DIM)
        q = q_ref[:, hs]
        s_c = _dot_nt(q, kc_ref[:, hs])
        s_w = _dot_nt(q, kl_ref[pl.ds(start, tk), hs]) + bias_ref[h]
        o = _softmax_pv([s_c, s_w], [_with_ones(vc_ref[:, hs]), _with_ones(vl_ref[pl.ds(start, tk), hs])], scale)
        o_ref[:, hs] = o.astype(o_ref.dtype)


def attn_b(rows, px, q0, k0, v0, bias, nh=2):
    grid_rows = rows.seq // GRID_W
    tq = NB_QROWS * GRID_W
    nqb = rows.seq // tq
    hw = nh * HEAD_DIM
    lat0 = rows.nc // rows.seq
    q_map = lambda g, i, b: (rows.nc // tq + b * nqb + i, q0 // hw + g)
    return pl.pallas_call(
        functools.partial(_attn_b_kernel, nh=nh, scale=HEAD_DIM ** -0.5, grid_rows=grid_rows),
        grid=(B_HEADS // nh, nqb, rows.batch),
        in_specs=[pl.BlockSpec((tq, hw), q_map),
                  pl.BlockSpec((rows.ctx_len, hw), lambda g, i, b: (b, k0 // hw + g)),
                  pl.BlockSpec((rows.seq, hw), lambda g, i, b: (lat0 + b, k0 // hw + g)),
                  pl.BlockSpec((rows.ctx_len, hw), lambda g, i, b: (b, v0 // hw + g)),
                  pl.BlockSpec((rows.seq, hw), lambda g, i, b: (lat0 + b, v0 // hw + g)),
                  pl.BlockSpec((nh, None, tq, NB_KROWS * GRID_W), lambda g, i, b: (g, i, 0, 0))],
        out_specs=pl.BlockSpec((tq, hw), lambda g, i, b: (rows.nc // tq + b * nqb + i, g)),
        out_shape=jax.ShapeDtypeStruct((rows.n, B_HEADS * HEAD_DIM), BF16),
        compiler_params=_params("arbitrary", "arbitrary", "arbitrary"),
        name="attn_b",
    )(px, px, px, px, px, bias)


def _merge_kernel(oa_ref, ob_ref, wa_ref, wb_ref, ga_ref, gb_ref, o_ref):
    a = _dot(oa_ref[...], wa_ref[...])
    b = _dot(ob_ref[...], wb_ref[...])
    y = _sigmoid(ga_ref[...].astype(F32)) * a + _sigmoid(gb_ref[...].astype(F32)) * b
    o_ref[...] = y.astype(o_ref.dtype)


def merge(oa, ob, wa, wb, l, px, ga0, gb0, row0, tm=1024, tn=1024):
    n, k = oa.shape
    d = wa.shape[2]
    t0 = row0 // tm
    return pl.pallas_call(
        _merge_kernel,
        grid=((n - row0) // tm, d // tn),
        in_specs=[pl.BlockSpec((tm, k), lambda i, j: (i + t0, 0)),
                  pl.BlockSpec((tm, k), lambda i, j: (i + t0, 0)),
                  pl.BlockSpec((None, k, tn), lambda i, j: (l, 0, j)),
                  pl.BlockSpec((None, k, tn), lambda i, j: (l, 0, j)),
                  pl.BlockSpec((tm, tn), lambda i, j: (i + t0, ga0 // tn + j)),
                  pl.BlockSpec((tm, tn), lambda i, j: (i + t0, gb0 // tn + j))],
        out_specs=pl.BlockSpec((tm, tn), lambda i, j: (i + t0, j)),
        out_shape=jax.ShapeDtypeStruct((n, d), BF16),
        compiler_params=_params("arbitrary", "arbitrary"),
        name="merge",
    )(oa, ob, wa, wb, px, px)


def _resid_mm_kernel(y_ref, w_ref, x_ref, g_ref, o_ref):
    o_ref[...] = x_ref[...] + g_ref[...] * _dot(y_ref[...], w_ref[...])


def resid_matmul(rows, y, w, l, x, gate, row0, tm, tn=1024):
    n, k = y.shape
    d = w.shape[2]
    t0 = row0 // tm
    return pl.pallas_call(
        _resid_mm_kernel,
        grid=((n - row0) // tm, d // tn),
        in_specs=[pl.BlockSpec((tm, k), lambda i, j: (i + t0, 0)),
                  pl.BlockSpec((None, k, tn), lambda i, j: (l, 0, j)),
                  pl.BlockSpec((tm, tn), lambda i, j: (i + t0, j)),
                  pl.BlockSpec((None, 1, tn), lambda i, j: (rows.mod_index((i + t0) * tm), 0, j))],
        out_specs=pl.BlockSpec((tm, tn), lambda i, j: (i + t0, j)),
        out_shape=jax.ShapeDtypeStruct(x.shape, x.dtype),
        input_output_aliases={2: 0},
        compiler_params=_params("arbitrary", "arbitrary"),
        name="resid_matmul",
    )(y, w, x, gate)


def _pack_words(v):
    half = v.shape[1] // 2
    bits = lax.bitcast_convert_type(v.astype(BF16).astype(F32), U32)
    return (bits[:, :half] >> 16) | bits[:, half:]


def _unpack_words(w):
    lo = lax.bitcast_convert_type(w << 16, F32)
    hi = lax.bitcast_convert_type(w & U32(0xFFFF0000), F32)
    return lo, hi


def _store_token_major(ref, words):
    rows, nch = words.shape[0], words.shape[1] // LANES
    for c in range(nch):
        ref[pl.ds(c, rows, stride=nch), :] = words[:, c * LANES:(c + 1) * LANES]


def _load_token_major(ref, rows, nch):
    return [ref[pl.ds(c, rows, stride=nch), :] for c in range(nch)]


def _route(logits, bias, n_experts):
    lane = lax.broadcasted_iota(jnp.int32, logits.shape, 1)
    lane_f = lane.astype(F32)
    per_group = n_experts // N_GROUPS
    valid = lane < n_experts
    scores = _sigmoid(logits)
    choice = jnp.where(valid, scores + bias, NEG_INF)
    big = float(LANES)

    def first_argmax(x):
        m = jnp.max(x, axis=-1, keepdims=True)
        idx = jnp.min(jnp.where(x == m, lane_f, big), axis=-1, keepdims=True)
        return m, idx

    in_group = [(lane >= g * per_group) & (lane < (g + 1) * per_group) for g in range(N_GROUPS)]
    group_score = []
    for g in range(N_GROUPS):
        xg = jnp.where(in_group[g], choice, NEG_INF)
        m1, i1 = first_argmax(xg)
        m2 = jnp.max(jnp.where(lane_f == i1, NEG_INF, xg), axis=-1, keepdims=True)
        group_score.append(m1 + m2)
    ok = jnp.zeros(logits.shape, F32)
    for g in range(N_GROUPS):
        ahead = jnp.zeros(group_score[g].shape, F32)
        for o in range(N_GROUPS):
            if o != g:
                wins = group_score[o] >= group_score[g] if o < g else group_score[o] > group_score[g]
                ahead = ahead + jnp.where(wins, 1.0, 0.0)
        ok = jnp.where(in_group[g], jnp.where(ahead < TOPK_GROUPS, 1.0, 0.0), ok)
    x = jnp.where(ok > 0.5, choice, NEG_INF)
    idxs, ws = [], []
    for _ in range(TOP_K):
        _, idx = first_argmax(x)
        hit = lane_f == idx
        idxs.append(idx)
        ws.append(jnp.sum(jnp.where(hit, scores, 0.0), axis=-1, keepdims=True))
        x = jnp.where(hit, NEG_INF, x)
    total = functools.reduce(jnp.add, ws)
    idx_tile = jnp.zeros(logits.shape, F32)
    w_tile = jnp.zeros(logits.shape, F32)
    for j in range(TOP_K):
        idx_tile = jnp.where(lane == j, idxs[j], idx_tile)
        w_tile = jnp.where(lane == j, ws[j] / total * ROUTED_SCALE, w_tile)
    return idx_tile.astype(jnp.int32), w_tile


def _norm_router_kernel(x_ref, g_ref, sc_ref, sh_ref, wh_ref, wl_ref, br_ref,
                        h_ref, htok_ref, idx_ref, wgt_ref, *, n_experts):
    y = _rmsnorm_rows(x_ref[...], g_ref[...])
    h = y * (1.0 + sc_ref[...]) + sh_ref[...]
    h_hi = h.astype(BF16)
    h_ref[...] = h_hi
    _store_token_major(htok_ref, _pack_words(h))
    h_lo = (h - h_hi.astype(F32)).astype(BF16)
    wh = wh_ref[...]
    logits = _dot(h_hi, wh) + _dot(h_lo, wh) + _dot(h_hi, wl_ref[...])
    idx_ref[...], wgt_ref[...] = _route(logits, br_ref[...], n_experts)


def norm_router(rows, x, g, sc, sh, w_router, b_router, row0, tm=256):
    n, d = x.shape
    e = w_router.shape[1]
    tok_rows = d // 2 // LANES
    t0 = row0 // tm
    wr = jnp.zeros((d, LANES), F32).at[:, :e].set(w_router)
    wr_hi = wr.astype(BF16)
    wr_lo = (wr - wr_hi.astype(F32)).astype(BF16)
    br = jnp.zeros((1, LANES), F32).at[0, :e].set(b_router)
    mod_map = lambda i: (rows.mod_index((i + t0) * tm), 0, 0)
    const = lambda i: (0, 0)
    row_map = lambda i: (i + t0, 0)
    return pl.pallas_call(
        functools.partial(_norm_router_kernel, n_experts=e),
        grid=((n - row0) // tm,),
        in_specs=[pl.BlockSpec((tm, d), row_map),
                  pl.BlockSpec((1, d), const),
                  pl.BlockSpec((None, 1, d), mod_map),
                  pl.BlockSpec((None, 1, d), mod_map),
                  pl.BlockSpec((d, LANES), const),
                  pl.BlockSpec((d, LANES), const),
                  pl.BlockSpec((1, LANES), const)],
        out_specs=[pl.BlockSpec((tm, d), row_map),
                   pl.BlockSpec((tm * tok_rows, LANES), row_map),
                   pl.BlockSpec((tm, LANES), row_map),
                   pl.BlockSpec((tm, LANES), row_map)],
        out_shape=[jax.ShapeDtypeStruct((n, d), BF16),
                   jax.ShapeDtypeStruct((n * tok_rows, LANES), U32),
                   jax.ShapeDtypeStruct((n, LANES), jnp.int32),
                   jax.ShapeDtypeStruct((n, LANES), F32)],
        compiler_params=_params("arbitrary"),
        name="norm_router",
    )(x, g.reshape(1, d), sc, sh, wr_hi, wr_lo, br)


def moe_plan(idx, wgt, tok0, n_experts, n_rows, tok_rows):
    t, p = idx.shape[0], idx.shape[0] * TOP_K
    n_tiles = p // MOE_TM + n_experts
    assert n_tiles % 2 == 0 and p % MOE_TM == 0 and 2 * MOE_TM <= n_rows
    flat_e = idx.reshape(p)
    order = jnp.argsort(flat_e, stable=True).astype(jnp.int32)
    counts = jnp.sum((flat_e[:, None] == jnp.arange(n_experts, dtype=jnp.int32)[None, :]).astype(jnp.int32), axis=0)
    tiles_per = (counts + MOE_TM - 1) // MOE_TM
    tile_end = jnp.cumsum(tiles_per)
    tile_start = tile_end - tiles_per
    pair_start = jnp.cumsum(counts) - counts
    tile = jnp.arange(n_tiles, dtype=jnp.int32)
    te = jnp.sum((tile[:, None] >= tile_end[None, :]).astype(jnp.int32), axis=1)
    te = jnp.minimum(te, n_experts - 1)
    lane = jnp.arange(MOE_TM, dtype=jnp.int32)[None, :]
    r = (tile - tile_start[te])[:, None] * MOE_TM + lane
    valid = (r < counts[te][:, None]) & (tile < tile_end[-1])[:, None]
    pair = order[jnp.clip(pair_start[te][:, None] + r, 0, p - 1)]
    tok = pair // TOP_K + tok0
    plane = pair % TOP_K
    slot = tile[:, None] * MOE_TM + lane
    src = jnp.where(valid, tok, tok0 + slot % t)
    dump = TOP_K * n_rows + (tile % 2)[:, None] * MOE_TM + lane
    dst = jnp.where(valid, plane * n_rows + tok, dump)
    wrow = jnp.where(valid, wgt.reshape(p)[pair], 0.0).reshape(n_tiles * MOE_TM, 1)
    src, dst = src * tok_rows, dst * tok_rows
    pad = jnp.zeros((n_tiles // 2, IDX_ROWS - 2, MOE_TM), jnp.int32)
    blk = jnp.concatenate([jnp.stack([src[0::2], src[1::2]], axis=1), pad,
                           jnp.stack([dst[0::2], dst[1::2]], axis=1), pad], axis=1).astype(jnp.int32)
    used_steps = ((tile_end[-1] + 1) // 2).astype(jnp.int32).reshape(1)
    return te, used_steps, blk, wrow


IDX_ROWS = 8


def _stage_pitch(tok_rows):
    return tok_rows if (tok_rows // 8) % 2 == 1 else tok_rows + 8


def _moe_expert_kernel(te_ref, ns_ref, idx_hbm, htok_hbm, wrow_ref,
                       wga_ref, wua_ref, wda_ref, wgb_ref, wub_ref, wdb_ref,
                       yt_hbm,
                       src_s, dst_s, in_a, in_b, out_a, out_b, idx_sem, in_sem, out_sem,
                       *, n_rows, tok_rows):
    del te_ref
    n_steps = ns_ref[0]
    k = pl.program_id(0)
    tm, nch = MOE_TM, tok_rows
    pitch = _stage_pitch(nch)
    stage_in, stage_out = (in_a, in_b), (out_a, out_b)
    dump0 = TOP_K * n_rows * nch

    def src_copy(step):
        return pltpu.make_async_copy(idx_hbm.at[step, pl.ds(0, IDX_ROWS), :], src_s, idx_sem.at[0])

    def dst_copy(step):
        return pltpu.make_async_copy(idx_hbm.at[step, pl.ds(IDX_ROWS, IDX_ROWS), :], dst_s, idx_sem.at[1])

    def issue_gathers(tile):
        for r in range(tm):
            off = pl.multiple_of(src_s[tile, r], nch)
            pltpu.make_async_copy(htok_hbm.at[pl.ds(off, nch), :],
                                  stage_in[tile].at[pl.ds(r * pitch, nch), :], in_sem.at[tile]).start()

    def wait_gathers(tile):
        pltpu.make_async_copy(htok_hbm.at[pl.ds(0, tm * nch), :], stage_in[tile].at[pl.ds(0, tm * nch), :],
                              in_sem.at[tile]).wait()

    def issue_scatters(tile):
        for r in range(tm):
            off = pl.multiple_of(dst_s[tile, r], nch)
            pltpu.make_async_copy(stage_out[tile].at[pl.ds(r * pitch, nch), :],
                                  yt_hbm.at[pl.ds(off, nch), :], out_sem.at[tile]).start(priority=r % 2)

    def bulk_scatter(tile):
        return pltpu.make_async_copy(stage_out[tile].at[pl.ds(0, tm * nch), :],
                                     yt_hbm.at[pl.ds(dump0 + tile * tm * nch, tm * nch), :], out_sem.at[tile])

    def load_rows(tile):
        los, his = [], []
        for c in range(nch):
            lo, hi = _unpack_words(stage_in[tile][pl.ds(c, tm, stride=pitch), :])
            los.append(lo.astype(BF16))
            his.append(hi.astype(BF16))
        return jnp.concatenate(los + his, axis=1)

    def compute(tile, wg_ref, wu_ref, wd_ref):
        x = load_rows(tile)
        hdim = wg_ref.shape[1]
        wgu = jnp.concatenate([wg_ref[...].astype(BF16), wu_ref[...].astype(BF16)], axis=1)
        gu = _dot(x, wgu)
        g, u = gu[:, :hdim], gu[:, hdim:]
        hid = (g * _sigmoid(g)) * u * wrow_ref[tile * tm:(tile + 1) * tm, :]
        y = _dot(hid.astype(BF16), wd_ref[...].astype(BF16))
        words = _pack_words(y)
        for c in range(nch):
            stage_out[tile][pl.ds(c, tm, stride=pitch), :] = words[:, c * LANES:(c + 1) * LANES]

    @pl.when(k < n_steps)
    def _():
        last = k == n_steps - 1
        next_step = jnp.where(last, 0, k + 1)

        @pl.when(k == 0)
        def _():
            first = src_copy(0)
            first.start()
            first.wait()
            issue_gathers(0)
            issue_gathers(1)
            src_copy(next_step).start()
            dst_copy(0).start()
            for tile in range(2):
                stage_out[tile][...] = jnp.zeros(stage_out[tile].shape, U32)
                bulk_scatter(tile).start()

        wait_gathers(0)
        bulk_scatter(0).wait()
        compute(0, wga_ref, wua_ref, wda_ref)
        dst_copy(k).wait()
        issue_scatters(0)
        src_copy(next_step).wait()
        issue_gathers(0)

        wait_gathers(1)
        bulk_scatter(1).wait()
        compute(1, wgb_ref, wub_ref, wdb_ref)
        issue_scatters(1)
        issue_gathers(1)

        @pl.when(jnp.logical_not(last))
        def _():
            src_copy(jnp.where(k + 2 < n_steps, k + 2, 0)).start()
            dst_copy(k + 1).start()

        @pl.when(last)
        def _():
            for tile in range(2):
                wait_gathers(tile)
                bulk_scatter(tile).wait()


def moe_experts(htok, te, used_steps, blk, wrow, wg, wu, wd, l, n_rows):
    n_steps = blk.shape[0]
    tok_rows = htok.shape[0] // n_rows
    _, _, d, hdim = wg.shape
    stage = pltpu.VMEM((MOE_TM * _stage_pitch(tok_rows), LANES), U32)
    w_in_spec = lambda off: pl.BlockSpec((None, None, d, hdim), lambda k, te, ns: (l, te[2 * k + off], 0, 0))
    w_out_spec = lambda off: pl.BlockSpec((None, None, hdim, d), lambda k, te, ns: (l, te[2 * k + off], 0, 0))
    return pl.pallas_call(
        functools.partial(_moe_expert_kernel, n_rows=n_rows, tok_rows=tok_rows),
        grid_spec=pltpu.PrefetchScalarGridSpec(
            num_scalar_prefetch=2,
            grid=(n_steps,),
            in_specs=[pl.BlockSpec(memory_space=pl.ANY),
                      pl.BlockSpec(memory_space=pl.ANY),
                      pl.BlockSpec((2 * MOE_TM, 1), lambda k, te, ns: (k, 0)),
                      w_in_spec(0), w_in_spec(0), w_out_spec(0),
                      w_in_spec(1), w_in_spec(1), w_out_spec(1)],
            out_specs=pl.BlockSpec(memory_space=pl.ANY),
            scratch_shapes=[pltpu.SMEM((IDX_ROWS, MOE_TM), jnp.int32),
                            pltpu.SMEM((IDX_ROWS, MOE_TM), jnp.int32),
                            stage, stage, stage, stage,
                            pltpu.SemaphoreType.DMA((2,)),
                            pltpu.SemaphoreType.DMA((2,)),
                            pltpu.SemaphoreType.DMA((2,))],
        ),
        out_shape=jax.ShapeDtypeStruct(((TOP_K + 1) * n_rows * tok_rows, LANES), U32),
        compiler_params=_params("arbitrary"),
        name="moe_experts",
    )(te, used_steps, blk, htok, wrow, wg, wu, wd, wg, wu, wd)


def _moe_combine_kernel(yt_ref, h_ref, wsg_ref, wsu_ref, wsd_ref, x_ref, g_ref, ng_ref, nsc_ref, nsh_ref,
                        o_ref, n_ref, acc_lo, acc_hi, *, tok_rows):
    tm = x_ref.shape[0]
    lo, hi = _unpack_words(yt_ref[0])
    for j in range(1, TOP_K):
        l2, h2 = _unpack_words(yt_ref[j])
        lo, hi = lo + l2, hi + h2
    acc_lo[...] = lo
    acc_hi[...] = hi
    routed = jnp.concatenate(_load_token_major(acc_lo, tm, tok_rows) + _load_token_major(acc_hi, tm, tok_rows), axis=1)
    h = h_ref[...]
    g = _dot(h, wsg_ref[...])
    u = _dot(h, wsu_ref[...])
    shared = _dot(((g * _sigmoid(g)) * u).astype(BF16), wsd_ref[...])
    o = x_ref[...] + g_ref[...] * (routed + shared)
    o_ref[...] = o
    y = _rmsnorm_rows(o, ng_ref[...])
    n_ref[...] = (y * (1.0 + nsc_ref[...]) + nsh_ref[...]).astype(n_ref.dtype)


def moe_combine(rows, yt, h, wsg, wsu, wsd, l, x, gate, next_g, next_sc, next_sh, next_dtype, row0, tm=128):
    n, d = x.shape
    tok_rows = d // 2 // LANES
    yt = yt.reshape(TOP_K + 1, n * tok_rows, LANES)
    s = wsg.shape[2]
    t0 = row0 // tm
    row_map = lambda i: (i + t0, 0)
    mod_map = lambda i: (rows.mod_index((i + t0) * tm), 0, 0)
    return pl.pallas_call(
        functools.partial(_moe_combine_kernel, tok_rows=tok_rows),
        grid=((n - row0) // tm,),
        in_specs=[pl.BlockSpec((TOP_K, tm * tok_rows, LANES), lambda i: (0, i + t0, 0)),
                  pl.BlockSpec((tm, d), row_map),
                  pl.BlockSpec((None, d, s), lambda i: (l, 0, 0)),
                  pl.BlockSpec((None, d, s), lambda i: (l, 0, 0)),
                  pl.BlockSpec((None, s, d), lambda i: (l, 0, 0)),
                  pl.BlockSpec((tm, d), row_map),
                  pl.BlockSpec((None, 1, d), mod_map),
                  pl.BlockSpec((1, d), lambda i: (0, 0)),
                  pl.BlockSpec((None, 1, d), mod_map),
                  pl.BlockSpec((None, 1, d), mod_map)],
        out_specs=[pl.BlockSpec((tm, d), row_map),
                   pl.BlockSpec((tm, d), lambda i: (i, 0))],
        out_shape=[jax.ShapeDtypeStruct(x.shape, x.dtype),
                   jax.ShapeDtypeStruct((n - row0, d), next_dtype)],
        scratch_shapes=[pltpu.VMEM((tm * tok_rows, LANES), F32), pltpu.VMEM((tm * tok_rows, LANES), F32)],
        input_output_aliases={5: 0},
        compiler_params=_params("arbitrary"),
        name="moe_combine",
    )(yt, h, wsg, wsu, wsd, x, gate, next_g.reshape(1, d), next_sc, next_sh)


def kernel(x, c, ctx, c_ctx, g_attn, g_ffn, w_mod_down, w_mod_up, b_mod_up, w_in, q_norm, k_norm, rpb, w_branch_a, w_branch_b, w_out, w_router, b_router, w_exp_gate, w_exp_up, w_exp_down, w_sh_gate, w_sh_up, w_sh_down, g_final):
    batch, seq, d = x.shape
    ctx_len = ctx.shape[1]
    depth = w_in.shape[0]
    n_experts = w_exp_gate.shape[1]
    rows = Rows(batch, ctx_len, seq)
    assert rows.nc % rows.seq == 0 and batch < MOD_ROWS

    a_q, a_kv, b_w = A_HEADS * HEAD_DIM, A_KV_HEADS * HEAD_DIM, B_HEADS * HEAD_DIM
    col_qa = 0
    col_qb = col_qa + a_q
    col_ka = col_qb + b_w
    col_va = col_ka + a_kv
    col_kb = col_va + a_kv
    col_vb = col_kb + b_w
    col_ga = col_vb + b_w
    col_gb = col_ga + d

    cvec = jnp.zeros((MOD_ROWS, d), F32).at[:batch].set(c).at[batch].set(c_ctx)
    mod = modulation_all(cvec, w_mod_down, w_mod_up, b_mod_up)

    w_in_b, w_a_b, w_b_b, w_out_b = (w.astype(BF16) for w in (w_in, w_branch_a, w_branch_b, w_out))
    wsg_b, wsu_b, wsd_b = (w.astype(BF16) for w in (w_sh_gate, w_sh_up, w_sh_down))

    prep_tm = 256
    cos, sin = rope_tables(rows, prep_tm)
    xs = jnp.concatenate([ctx.reshape(rows.nc, d), x.reshape(rows.nx, d)], axis=0)

    mods = [[mod[l, :batch + 1, k * d:(k + 1) * d][:, None, :] for k in range(N_MOD)] for l in range(depth)]
    no_mod = jnp.zeros((batch + 1, 1, d), F32)
    h1 = norm_mod(rows, xs, g_attn[0], mods[0][1], mods[0][0], 0)
    for l in range(depth):
        update_ctx = l < depth - 1
        row0 = 0 if update_ctx else rows.nc
        sh1, sc1, gt1, sh2, sc2, gt2 = mods[l]

        px = matmul(h1, w_in_b, l)
        qa, ka = qk_prep(rows, px, (col_qa, a_q), (col_ka, a_kv), q_norm[l], k_norm[l], cos, sin, prep_tm)
        oa = attn_a(rows, qa, ka, px, col_va)
        ob = attn_b(rows, px, col_qb, col_kb, col_vb, nb_bias(rpb[l], seq // GRID_W))
        if update_ctx:
            oa = attn_ctx(rows, qa, 0, ka, 0, px, col_va, oa, A_HEADS, A_KV_HEADS)
            ob = attn_ctx(rows, px, col_qb, px, col_kb, px, col_vb, ob, B_HEADS, B_HEADS)
        y = merge(oa, ob, w_a_b, w_b_b, l, px, col_ga, col_gb, row0)
        xs = resid_matmul(rows, y, w_out_b, l, xs, gt1, row0, tm=1024)

        h2, htok, idx_t, wgt_t = norm_router(rows, xs, g_ffn[l], sc2, sh2, w_router[l], b_router[l], row0)
        te, used_steps, blk, wrow = moe_plan(idx_t[row0:, :TOP_K], wgt_t[row0:, :TOP_K], row0, n_experts, rows.n, d // 2 // LANES)
        yt = moe_experts(htok, te, used_steps, blk, wrow, w_exp_gate, w_exp_up, w_exp_down, l, rows.n)
        if update_ctx:
            nxt = (g_attn[l + 1], mods[l + 1][1], mods[l + 1][0], BF16)
        else:
            nxt = (g_final, no_mod, no_mod, F32)
        xs, h1 = moe_combine(rows, yt, h2, wsg_b, wsu_b, wsd_b, l, xs, gt2, *nxt, row0)

    return h1.reshape(batch, seq, d)
```
